```python
import math, functools
import jax, jax.numpy as jnp
from jax import lax
import numpy as np

D_MODEL = 1024
BATCH = 4
SEQ = 4096
DEPTH = 2
DEC_BATCH = 128
DEC_SEQ = 1
PAST_LEN = 16384
PAGE_SIZE = 128

N_A_LAYERS = DEPTH // 2
N_B_LAYERS = DEPTH - N_A_LAYERS
SSM_GROUP = 16
SSM_GROUPS = D_MODEL // SSM_GROUP
SSM_STATE = 64
DT_MIN = 0.001
DT_MAX = 0.1
N_HEADS = 16
QK_NOPE = 64
QK_ROPE = 32
V_HEAD = 64
Q_LORA = 512
KV_LORA = 256
ROPE_THETA = 10000.0
Q_BLOCK = 128
SM_SCALE = 1.0 / math.sqrt(QK_NOPE + QK_ROPE)
NEG_INF = -1e30
PEER_HEADS = 8
N_KEYS = 128
N_EXPERTS = N_KEYS * N_KEYS
PEER_D_KEY = 256
PEER_TOPK = 16
PEER_TOKEN_BLOCK = 128
EPS = 1e-6

kernel_name = "yoco_s5_mla_peer_adaln_step"


def rmsnorm(x, g):
    xf = x.astype(jnp.float32)
    y = xf * lax.rsqrt(jnp.mean(xf * xf, axis=-1, keepdims=True) + EPS)
    return (y * g.astype(jnp.float32)).astype(x.dtype)


def modulate(h, shift, scale):
    return h * (1.0 + scale[:, None, :]) + shift[:, None, :]


def split_mod(c, w, b, n):
    mod = jnp.einsum("bd,de->be", jax.nn.silu(c), w) + b
    return jnp.split(mod, n, axis=-1)


def rope(x, pos):
    half = QK_ROPE // 2
    inv = ROPE_THETA ** (-jnp.arange(0, QK_ROPE, 2, dtype=jnp.float32) / QK_ROPE)
    ang = pos.astype(jnp.float32)[..., None] * inv
    cos, sin = jnp.cos(ang), jnp.sin(ang)
    xf = x.astype(jnp.float32)
    x1, x2 = xf[..., :half], xf[..., half:]
    return jnp.concatenate([x1 * cos - x2 * sin, x1 * sin + x2 * cos], axis=-1).astype(x.dtype)


def _linear_combine(left, right):
    a_l, b_l = left
    a_r, b_r = right
    return a_r * a_l, a_r * b_l + b_r


def s5_mixer(u, h0_re, h0_im, a_re, a_im, log_dt, b_re, b_im, c_re, c_im, d_skip, w_glu):
    f32 = jnp.float32
    bsz, length, _ = u.shape
    uf = u.astype(f32).reshape(bsz, length, SSM_GROUPS, SSM_GROUP)
    lam = lax.complex(a_re.astype(f32), a_im.astype(f32))
    dt = jnp.exp(log_dt.astype(f32))[:, None]
    lam_bar = jnp.exp(lam * dt)
    b_mat = lax.complex(b_re.astype(f32), b_im.astype(f32))
    b_bar = ((lam_bar - 1.0) / lam)[:, :, None] * b_mat
    c_mat = lax.complex(c_re.astype(f32), c_im.astype(f32))
    bu = jnp.einsum("blgp,gnp->blgn", uf.astype(jnp.complex64), b_bar)
    h0 = lax.complex(h0_re.astype(f32), h0_im.astype(f32))
    bu = bu.at[:, 0].add(lam_bar * h0)
    a_seq = jnp.broadcast_to(lam_bar, bu.shape)
    _, h = lax.associative_scan(_linear_combine, (a_seq, bu), axis=1)
    y = jnp.real(jnp.einsum("blgn,gpn->blgp", h, c_mat)) + d_skip.astype(f32).reshape(SSM_GROUPS, SSM_GROUP) * uf
    z = jax.nn.gelu(y.reshape(bsz, length, D_MODEL))
    a, g = jnp.split(z @ w_glu.astype(f32), 2, axis=-1)
    out = (a * jax.nn.sigmoid(g)).astype(u.dtype)
    h_last = h[:, -1]
    return out, jnp.real(h_last).astype(h0_re.dtype), jnp.imag(h_last).astype(h0_im.dtype)


def shared_kv(x, c, positions, w_mod_kv, b_mod_kv, g_kv_norm, w_dkv, g_ckv, w_kr):
    sh, sc = split_mod(c, w_mod_kv, b_mod_kv, 2)
    h = modulate(rmsnorm(x, g_kv_norm), sh, sc)
    ckv = rmsnorm(h @ w_dkv, g_ckv)
    kr = rope(h @ w_kr, positions)
    return ckv, kr


def mla_query(h, positions, w_dq, g_cq, w_uq, w_ukv):
    bsz, length, _ = h.shape
    cq = rmsnorm(h @ w_dq, g_cq)
    q = (cq @ w_uq).reshape(bsz, length, N_HEADS, QK_NOPE + QK_ROPE)
    q_nope, q_pe = q[..., :QK_NOPE], q[..., QK_NOPE:]
    q_pe = rope(q_pe, positions[:, None])
    q_lat = jnp.einsum("blhd,chd->blhc", q_nope, w_ukv[..., :QK_NOPE])
    return q_lat, q_pe


def mla_out(o_lat, w_ukv, w_o):
    o = jnp.einsum("blhc,chd->blhd", o_lat, w_ukv[..., QK_NOPE:])
    return o.reshape(o.shape[0], o.shape[1], N_HEADS * V_HEAD) @ w_o


def prompt_attention(q_lat, q_pe, ckv, kr):
    f32 = jnp.float32
    bsz, length = q_lat.shape[:2]
    nb = length // Q_BLOCK
    ql = q_lat.reshape(bsz, nb, Q_BLOCK, N_HEADS, KV_LORA).transpose(1, 0, 2, 3, 4)
    qp = q_pe.reshape(bsz, nb, Q_BLOCK, N_HEADS, QK_ROPE).transpose(1, 0, 2, 3, 4)
    kpos = jnp.arange(length)

    def block(args):
        i, qlb, qpb = args
        s = (jnp.einsum("bqhc,bkc->bhqk", qlb, ckv, preferred_element_type=f32)
             + jnp.einsum("bqhr,bkr->bhqk", qpb, kr, preferred_element_type=f32)) * SM_SCALE
        qpos = i * Q_BLOCK + jnp.arange(Q_BLOCK)
        s = jnp.where(kpos[None, :] <= qpos[:, None], s, NEG_INF)
        p = jax.nn.softmax(s, axis=-1).astype(ckv.dtype)
        return jnp.einsum("bhqk,bkc->bqhc", p, ckv, preferred_element_type=f32).astype(q_lat.dtype)

    o = lax.map(block, (jnp.arange(nb), ql, qp))
    return o.transpose(1, 0, 2, 3, 4).reshape(bsz, length, N_HEADS, KV_LORA)


def sample_attention(q_lat, q_pe, ckv_new, kr_new, cache_ckv, cache_kr, page_table):
    f32 = jnp.float32
    nb, s = q_lat.shape[:2]

    def scores(kc, kp):
        return (jnp.einsum("bqhc,bkc->bhqk", q_lat, kc, preferred_element_type=f32)
                + jnp.einsum("bqhr,bkr->bhqk", q_pe, kp, preferred_element_type=f32)) * SM_SCALE

    def update(carry, sc, kc):
        m, l, acc = carry
        m_new = jnp.maximum(m, sc.max(axis=-1))
        corr = jnp.exp(m - m_new)
        p = jnp.exp(sc - m_new[..., None])
        l = l * corr + p.sum(axis=-1)
        acc = acc * corr[..., None] + jnp.einsum("bhqk,bkc->bhqc", p, kc.astype(f32))
        return m_new, l, acc

    def page_step(carry, pages):
        kc = cache_ckv[pages]
        kp = cache_kr[pages]
        return update(carry, scores(kc, kp), kc), None

    init = (jnp.full((nb, N_HEADS, s), NEG_INF, f32),
            jnp.zeros((nb, N_HEADS, s), f32),
            jnp.zeros((nb, N_HEADS, s, KV_LORA), f32))
    carry, _ = lax.scan(page_step, init, page_table.T)
    causal = jnp.arange(s)[None, :] <= jnp.arange(s)[:, None]
    sc_new = jnp.where(causal, scores(ckv_new, kr_new), NEG_INF)
    _, l, acc = update(carry, sc_new, ckv_new)
    o = acc / l[..., None]
    return o.transpose(0, 2, 1, 3).astype(q_lat.dtype)


def peer(h, w_q, sub_keys, u_emb, v_emb):
    f32 = jnp.float32
    shape = h.shape
    xt = h.reshape(-1, D_MODEL)
    n_tok = xt.shape[0]
    nb = -(-n_tok // PEER_TOKEN_BLOCK)
    xt = jnp.pad(xt, ((0, nb * PEER_TOKEN_BLOCK - n_tok), (0, 0))).reshape(nb, PEER_TOKEN_BLOCK, D_MODEL)

    def block(xb):
        q = (xb @ w_q).reshape(PEER_TOKEN_BLOCK, PEER_HEADS, 2, PEER_D_KEY // 2)
        s = jnp.einsum("thsd,hsnd->thsn", q, sub_keys, preferred_element_type=f32)
        top_s, top_i = lax.top_k(s, PEER_TOPK)
        cand_s = top_s[:, :, 0, :, None] + top_s[:, :, 1, None, :]
        cand_i = top_i[:, :, 0, :, None] * N_KEYS + top_i[:, :, 1, None, :]
        best_s, best_pos = lax.top_k(cand_s.reshape(PEER_TOKEN_BLOCK, PEER_HEADS, PEER_TOPK * PEER_TOPK), PEER_TOPK)
        idx = jnp.take_along_axis(cand_i.reshape(PEER_TOKEN_BLOCK, PEER_HEADS, PEER_TOPK * PEER_TOPK), best_pos, axis=-1)
        g = jax.nn.softmax(best_s, axis=-1)
        u = u_emb[idx]
        v = v_emb[idx]
        act = jax.nn.gelu(jnp.einsum("td,thkd->thk", xb, u, preferred_element_type=f32))
        return jnp.einsum("thk,thkd->td", (g * act).astype(v.dtype), v)

    y = lax.map(block, xt).reshape(nb * PEER_TOKEN_BLOCK, D_MODEL)[:n_tok]
    return y.reshape(shape).astype(h.dtype)


def run_trunk(x, c, positions, h0_re, h0_im, attend,
              w_mod, b_mod, g_norm,
              ssm_a_re, ssm_a_im, ssm_log_dt, ssm_b_re, ssm_b_im, ssm_c_re, ssm_c_im, ssm_d, ssm_w_glu,
              w_mod_kv, b_mod_kv, g_kv_norm, w_dkv, g_ckv, w_kr, w_ukv,
              w_dq, g_cq, w_uq, w_o,
              peer_w_q, peer_keys, peer_u, peer_v,
              w_mod_final, b_mod_final, g_final):
    ssm_re, ssm_im = [], []
    ckv = kr = None
    for layer in range(DEPTH):
        sh1, sc1, ga1, sh2, sc2, ga2 = split_mod(c, w_mod[layer], b_mod[layer], 6)
        if layer == N_A_LAYERS:
            ckv, kr = shared_kv(x, c, positions, w_mod_kv, b_mod_kv, g_kv_norm, w_dkv, g_ckv, w_kr)
        h = modulate(rmsnorm(x, g_norm[layer, 0]), sh1, sc1)
        if layer < N_A_LAYERS:
            y, hr, hi = s5_mixer(h, h0_re[layer], h0_im[layer], ssm_a_re[layer], ssm_a_im[layer],
                                 ssm_log_dt[layer], ssm_b_re[layer], ssm_b_im[layer],
                                 ssm_c_re[layer], ssm_c_im[layer], ssm_d[layer], ssm_w_glu[layer])
            ssm_re.append(hr)
            ssm_im.append(hi)
        else:
            j = layer - N_A_LAYERS
            q_lat, q_pe = mla_query(h, positions, w_dq[j], g_cq[j], w_uq[j], w_ukv)
            y = mla_out(attend(q_lat, q_pe, ckv, kr), w_ukv, w_o[j])
        x = x + ga1[:, None, :] * y
        h = modulate(rmsnorm(x, g_norm[layer, 1]), sh2, sc2)
        x = x + ga2[:, None, :] * peer(h, peer_w_q[layer], peer_keys[layer], peer_u[layer], peer_v[layer])
    shf, scf = split_mod(c, w_mod_final, b_mod_final, 2)
    y_out = modulate(rmsnorm(x, g_final), shf, scf)
    return y_out, jnp.stack(ssm_re), jnp.stack(ssm_im), ckv, kr


def setup_inputs(seed: int = 0) -> dict:
    key = jax.random.key(seed)
    ks = iter(jax.random.split(key, 64))

    def normal(shape, scale):
        return jax.random.normal(next(ks), shape, jnp.float32) * scale

    n_pages = PAST_LEN // PAGE_SIZE
    used = DEC_BATCH * n_pages
    n_phys = used + max(1, used // 4)
    perm = jax.random.permutation(next(ks), n_phys)
    page_table = perm[:used].reshape(DEC_BATCH, n_pages).astype(jnp.int32)

    ssm_shape = (N_A_LAYERS, SSM_GROUPS, SSM_STATE)
    a_im = jnp.pi * jnp.arange(SSM_STATE, dtype=jnp.float32)
    u01 = jax.random.uniform(next(ks), (N_A_LAYERS, SSM_GROUPS), jnp.float32)
    log_dt = math.log(DT_MIN) + u01 * (math.log(DT_MAX) - math.log(DT_MIN))
    d = D_MODEL
    return {
        "x_prompt": normal((BATCH, SEQ, d), 1.0),
        "x_sample": normal((DEC_BATCH, DEC_SEQ, d), 1.0),
        "state_ssm_re": normal((N_A_LAYERS, DEC_BATCH, SSM_GROUPS, SSM_STATE), 0.5),
        "state_ssm_im": normal((N_A_LAYERS, DEC_BATCH, SSM_GROUPS, SSM_STATE), 0.5),
        "cache_kv_latent": normal((n_phys, PAGE_SIZE, KV_LORA), 1.0),
        "cache_k_rope": normal((n_phys, PAGE_SIZE, QK_ROPE), 1.0),
        "page_table": page_table,
        "c_prompt": normal((BATCH, d), 1.0),
        "c_sample": normal((DEC_BATCH, d), 1.0),
        "w_mod": normal((DEPTH, d, 6 * d), 0.5 * d ** -0.5),
        "b_mod": normal((DEPTH, 6 * d), 0.01),
        "g_norm": 1.0 + normal((DEPTH, 2, d), 0.02),
        "ssm_a_re": -0.5 + normal(ssm_shape, 0.01),
        "ssm_a_im": a_im + normal(ssm_shape, 0.01),
        "ssm_log_dt": log_dt,
        "ssm_b_re": normal((N_A_LAYERS, SSM_GROUPS, SSM_STATE, SSM_GROUP), (2 * SSM_GROUP) ** -0.5),
        "ssm_b_im": normal((N_A_LAYERS, SSM_GROUPS, SSM_STATE, SSM_GROUP), (2 * SSM_GROUP) ** -0.5),
        "ssm_c_re": normal((N_A_LAYERS, SSM_GROUPS, SSM_GROUP, SSM_STATE), (2 * SSM_STATE) ** -0.5),
        "ssm_c_im": normal((N_A_LAYERS, SSM_GROUPS, SSM_GROUP, SSM_STATE), (2 * SSM_STATE) ** -0.5),
        "ssm_d": normal((N_A_LAYERS, d), 1.0),
        "ssm_w_glu": normal((N_A_LAYERS, d, 2 * d), d ** -0.5),
        "w_mod_kv": normal((d, 2 * d), 0.5 * d ** -0.5),
        "b_mod_kv": normal((2 * d,), 0.01),
        "g_kv_norm": 1.0 + normal((d,), 0.02),
        "w_dkv": normal((d, KV_LORA), d ** -0.5),
        "g_ckv": 1.0 + normal((KV_LORA,), 0.02),
        "w_kr": normal((d, QK_ROPE), d ** -0.5),
        "w_ukv": normal((KV_LORA, N_HEADS, QK_NOPE + V_HEAD), KV_LORA ** -0.5),
        "w_dq": normal((N_B_LAYERS, d, Q_LORA), d ** -0.5),
        "g_cq": 1.0 + normal((N_B_LAYERS, Q_LORA), 0.02),
        "w_uq": normal((N_B_LAYERS, Q_LORA, N_HEADS * (QK_NOPE + QK_ROPE)), Q_LORA ** -0.5),
        "w_o": normal((N_B_LAYERS, N_HEADS * V_HEAD, d), (N_HEADS * V_HEAD) ** -0.5),
        "peer_w_q": normal((DEPTH, d, PEER_HEADS * PEER_D_KEY), d ** -0.5),
        "peer_keys": normal((DEPTH, PEER_HEADS, 2, N_KEYS, PEER_D_KEY // 2), (PEER_D_KEY // 2) ** -0.5),
        "peer_u": normal((DEPTH, N_EXPERTS, d), d ** -0.5),
        "peer_v": normal((DEPTH, N_EXPERTS, d), 0.5),
        "w_mod_final": normal((d, 2 * d), 0.5 * d ** -0.5),
        "b_mod_final": normal((2 * d,), 0.01),
        "g_final": 1.0 + normal((d,), 0.02),
    }


def reference(x_prompt, x_sample, state_ssm_re, state_ssm_im, cache_kv_latent, cache_k_rope, page_table,
              c_prompt, c_sample,
              w_mod, b_mod, g_norm,
              ssm_a_re, ssm_a_im, ssm_log_dt, ssm_b_re, ssm_b_im, ssm_c_re, ssm_c_im, ssm_d, ssm_w_glu,
              w_mod_kv, b_mod_kv, g_kv_norm, w_dkv, g_ckv, w_kr, w_ukv,
              w_dq, g_cq, w_uq, w_o,
              peer_w_q, peer_keys, peer_u, peer_v,
              w_mod_final, b_mod_final, g_final):
    weights = (w_mod, b_mod, g_norm,
               ssm_a_re, ssm_a_im, ssm_log_dt, ssm_b_re, ssm_b_im, ssm_c_re, ssm_c_im, ssm_d, ssm_w_glu,
               w_mod_kv, b_mod_kv, g_kv_norm, w_dkv, g_ckv, w_kr, w_ukv,
               w_dq, g_cq, w_uq, w_o,
               peer_w_q, peer_keys, peer_u, peer_v,
               w_mod_final, b_mod_final, g_final)
    pos_p = jnp.arange(x_prompt.shape[1], dtype=jnp.int32)
    h0_p = jnp.zeros((N_A_LAYERS, x_prompt.shape[0], SSM_GROUPS, SSM_STATE), state_ssm_re.dtype)
    y_prompt, ssm_re_p, ssm_im_p, kv_lat_p, k_rope_p = run_trunk(
        x_prompt, c_prompt, pos_p, h0_p, h0_p, prompt_attention, *weights)
    pos_s = PAST_LEN + jnp.arange(x_sample.shape[1], dtype=jnp.int32)
    attend_s = functools.partial(sample_attention, cache_ckv=cache_kv_latent, cache_kr=cache_k_rope,
                                 page_table=page_table)
    y_sample, ssm_re_s, ssm_im_s, kv_lat_s, k_rope_s = run_trunk(
        x_sample, c_sample, pos_s, state_ssm_re, state_ssm_im, attend_s, *weights)
    return (y_prompt, y_sample, ssm_re_p, ssm_im_p, kv_lat_p, k_rope_p, ssm_re_s, ssm_im_s, kv_lat_s, k_rope_s)
```

```python
import functools
import math

import jax
import jax.numpy as jnp
from jax import lax
from jax.experimental import pallas as pl
from jax.experimental.pallas import tpu as pltpu

F32 = jnp.float32
BF16 = jnp.bfloat16
EPS = 1e-6
ROPE_THETA = 10000.0
PEER_TOPK = 16
NEG_INF = -1e30
LANES = 128
SSM_GROUPS_PER_BLOCK = 16
VMEM_LIMIT = 56 * 1024 * 1024

_NN = (((1,), (0,)), ((), ()))
_NT = (((1,), (1,)), ((), ()))


def _dot(a, b, dims=_NN):
    return lax.dot_general(a, b, dims, preferred_element_type=F32)


def _split(a):
    hi = a.astype(BF16)
    lo = (a - hi.astype(F32)).astype(BF16)
    return hi, lo


def _dot3(a, b, dims=_NN):
    ah, al = _split(a)
    bh, bl = _split(b)
    return _dot(ah, bh, dims) + (_dot(ah, bl, dims) + _dot(al, bh, dims))


def _dot3w(a, wh, wl, dims=_NN):
    ah, al = _split(a)
    return _dot(ah, wh, dims) + (_dot(ah, wl, dims) + _dot(al, wh, dims))


def _sigmoid(x):
    return 1.0 / (1.0 + jnp.exp(-x))


def _rms(x, g):
    return x * lax.rsqrt(jnp.mean(x * x, axis=-1, keepdims=True) + EPS) * g


def _norm_mod(x, g, sh, sc):
    return _rms(x, g) * (1.0 + sc) + sh


def _params(*sem):
    return pltpu.CompilerParams(dimension_semantics=sem, vmem_limit_bytes=VMEM_LIMIT)


def _full(a):
    nd = a.ndim
    return pl.BlockSpec(a.shape, lambda *_: (0,) * nd)


def _tok(width, tb):
    return pl.BlockSpec((1, tb, width), lambda b, i, *_: (b, i, 0))


def _modspec(m, tb):
    if m.shape[1] == 1:
        return pl.BlockSpec((1, 1, m.shape[2]), lambda b, i, *_: (b, 0, 0))
    return pl.BlockSpec((1, tb, m.shape[2]), lambda b, i, *_: (b, i, 0))


def _tok_block(lg, cap=512):
    tb = min(lg, cap)
    assert lg % tb == 0
    return tb


def _hi_lo(w):
    hi = w.astype(BF16)
    return hi, (w - hi.astype(F32)).astype(BF16)


def _mod_kernel(c_ref, wh_ref, wl_ref, b_ref, o_ref):
    c = c_ref[...]
    o_ref[...] = _dot3w(c * _sigmoid(c), wh_ref[...], wl_ref[...]) + b_ref[...]


def _mod_linear(c, w, b):
    bc, d = c.shape
    n = w.shape[1]
    tn = min(n, 1024)
    assert n % tn == 0
    wh, wl = _hi_lo(w)
    return pl.pallas_call(
        _mod_kernel,
        grid=(n // tn,),
        in_specs=[pl.BlockSpec((bc, d), lambda j: (0, 0)),
                  pl.BlockSpec((d, tn), lambda j: (0, j)),
                  pl.BlockSpec((d, tn), lambda j: (0, j)),
                  pl.BlockSpec((1, tn), lambda j: (0, j))],
        out_specs=pl.BlockSpec((bc, tn), lambda j: (0, j)),
        out_shape=jax.ShapeDtypeStruct((bc, n), F32),
        compiler_params=_params("parallel"),
        name="mod_linear",
    )(c, wh, wl, b.reshape(1, n))


def _s5_prep(a_re, a_im, log_dt, b_re, b_im, c_re, c_im):
    g, n, p = b_re.shape
    gb = min(g, SSM_GROUPS_PER_BLOCK)
    assert g % gb == 0
    k = g // gb
    dt = jnp.exp(log_dt.astype(F32))[:, None]
    a_re = a_re.astype(F32)
    a_im = a_im.astype(F32)
    er = jnp.exp(a_re * dt)
    lr = er * jnp.cos(a_im * dt)
    li = er * jnp.sin(a_im * dt)
    den = a_re * a_re + a_im * a_im
    kr = ((lr - 1.0) * a_re + li * a_im) / den
    ki = (li * a_re - (lr - 1.0) * a_im) / den
    bbr = kr[..., None] * b_re - ki[..., None] * b_im
    bbi = kr[..., None] * b_im + ki[..., None] * b_re
    eye = jnp.eye(gb, dtype=F32)

    def blk_in(b):
        return jnp.einsum("kgnp,gh->kgphn", b.reshape(k, gb, n, p), eye).reshape(k, gb * p, gb * n)

    def blk_out(c):
        return jnp.einsum("kgpn,gh->kgnhp", c.reshape(k, gb, p, n), eye).reshape(k, gb * n, gb * p)

    b_blk = jnp.concatenate([blk_in(bbr), blk_in(bbi)], axis=-1)
    c_blk = jnp.concatenate([blk_out(c_re.astype(F32)), -blk_out(c_im.astype(F32))], axis=1)
    bh, bl = _hi_lo(b_blk)
    return dict(bh=bh, bl=bl, c=c_blk.astype(BF16), lam_re=lr.reshape(1, g * n), lam_im=li.reshape(1, g * n))


def _s5_input(h, bh_ref, bl_ref, k, w):
    hs = h[:, k * w:(k + 1) * w]
    return _dot3w(hs, bh_ref[k], bl_ref[k])


def _s5_prompt_kernel(x_ref, g_ref, sh_ref, sc_ref, bh_ref, bl_ref, c_ref, lre_ref, lim_ref, d_ref,
                      z_ref, sre_ref, sim_ref, bu_re, bu_im, st_re, st_im, *, nblk, tc):
    j = pl.program_id(1)

    @pl.when(j == 0)
    def _():
        st_re[...] = jnp.zeros_like(st_re)
        st_im[...] = jnp.zeros_like(st_im)

    h = _norm_mod(x_ref[0], g_ref[...], sh_ref[0], sc_ref[0])
    w = h.shape[1] // nblk
    nw = st_re.shape[1] // nblk
    for k in range(nblk):
        cols = slice(k * nw, (k + 1) * nw)
        bu = _s5_input(h, bh_ref, bl_ref, k, w)
        bu_re[...] = bu[:, :nw]
        bu_im[...] = bu[:, nw:]
        lr = lre_ref[:, cols]
        li = lim_ref[:, cols]

        def step(t, carry):
            sr, si = carry
            nr = lr * sr - li * si + bu_re[pl.ds(t, 1), :]
            ni = lr * si + li * sr + bu_im[pl.ds(t, 1), :]
            bu_re[pl.ds(t, 1), :] = nr
            bu_im[pl.ds(t, 1), :] = ni
            return nr, ni

        sr, si = lax.fori_loop(0, tc, step, (st_re[:, cols], st_im[:, cols]))
        st_re[:, cols] = sr
        st_im[:, cols] = si
        y = (_dot(bu_re[...].astype(BF16), c_ref[k, :nw, :])
             + _dot(bu_im[...].astype(BF16), c_ref[k, nw:, :]))
        hk = h[:, k * w:(k + 1) * w]
        z_ref[0, :, k * w:(k + 1) * w] = jax.nn.gelu(y + d_ref[:, k * w:(k + 1) * w] * hk)
    sre_ref[0] = st_re[...]
    sim_ref[0] = st_im[...]


def _s5_prompt(x, g, sh, sc, prm, d_skip):
    bsz, lg, d = x.shape
    nblk = prm["bh"].shape[0]
    gn = prm["lam_re"].shape[1]
    tc = _tok_block(lg, 256)
    kern = functools.partial(_s5_prompt_kernel, nblk=nblk, tc=tc)
    st_spec = pl.BlockSpec((1, 1, gn), lambda b, j: (b, 0, 0))
    return pl.pallas_call(
        kern,
        grid=(bsz, lg // tc),
        in_specs=[_tok(d, tc), _full(g), _modspec(sh, tc), _modspec(sc, tc),
                  _full(prm["bh"]), _full(prm["bl"]), _full(prm["c"]),
                  _full(prm["lam_re"]), _full(prm["lam_im"]), _full(d_skip)],
        out_specs=[_tok(d, tc), st_spec, st_spec],
        out_shape=[jax.ShapeDtypeStruct((bsz, lg, d), F32),
                   jax.ShapeDtypeStruct((bsz, 1, gn), F32),
                   jax.ShapeDtypeStruct((bsz, 1, gn), F32)],
        scratch_shapes=[pltpu.VMEM((tc, gn // nblk), F32), pltpu.VMEM((tc, gn // nblk), F32),
                        pltpu.VMEM((1, gn), F32), pltpu.VMEM((1, gn), F32)],
        compiler_params=_params("parallel", "arbitrary"),
        name="s5_prompt",
    )(x, g, sh, sc, prm["bh"], prm["bl"], prm["c"], prm["lam_re"], prm["lam_im"], d_skip)


def _s5_step_kernel(x_ref, g_ref, sh_ref, sc_ref, h0re_ref, h0im_ref, bh_ref, bl_ref, c_ref,
                    lre_ref, lim_ref, d_ref, z_ref, sre_ref, sim_ref, *, nblk):
    h = _norm_mod(x_ref[0], g_ref[...], sh_ref[0], sc_ref[0])
    w = h.shape[1] // nblk
    nw = lre_ref.shape[1] // nblk
    for k in range(nblk):
        cols = slice(k * nw, (k + 1) * nw)
        bu = _s5_input(h, bh_ref, bl_ref, k, w)
        lr = lre_ref[:, cols]
        li = lim_ref[:, cols]
        pr = h0re_ref[:, cols]
        pi = h0im_ref[:, cols]
        nr = lr * pr - li * pi + bu[:, :nw]
        ni = lr * pi + li * pr + bu[:, nw:]
        sre_ref[:, cols] = nr
        sim_ref[:, cols] = ni
        y = _dot(nr.astype(BF16), c_ref[k, :nw, :]) + _dot(ni.astype(BF16), c_ref[k, nw:, :])
        hk = h[:, k * w:(k + 1) * w]
        z_ref[0, :, k * w:(k + 1) * w] = jax.nn.gelu(y + d_ref[:, k * w:(k + 1) * w] * hk)


def _s5_step(x, g, sh, sc, h0_re, h0_im, prm, d_skip):
    _, rows, d = x.shape
    nblk = prm["bh"].shape[0]
    gn = prm["lam_re"].shape[1]
    tb = _tok_block(rows, 128)
    st_spec = pl.BlockSpec((tb, gn), lambda b, i: (i, 0))
    return pl.pallas_call(
        functools.partial(_s5_step_kernel, nblk=nblk),
        grid=(1, rows // tb),
        in_specs=[_tok(d, tb), _full(g), _modspec(sh, tb), _modspec(sc, tb), st_spec, st_spec,
                  _full(prm["bh"]), _full(prm["bl"]), _full(prm["c"]),
                  _full(prm["lam_re"]), _full(prm["lam_im"]), _full(d_skip)],
        out_specs=[_tok(d, tb), st_spec, st_spec],
        out_shape=[jax.ShapeDtypeStruct((1, rows, d), F32),
                   jax.ShapeDtypeStruct((rows, gn), F32),
                   jax.ShapeDtypeStruct((rows, gn), F32)],
        compiler_params=_params("parallel", "parallel"),
        name="s5_step",
    )(x, g, sh, sc, h0_re, h0_im, prm["bh"], prm["bl"], prm["c"], prm["lam_re"], prm["lam_im"], d_skip)


def _glu_res_kernel(z_ref, w_ref, x_ref, ga_ref, o_ref):
    y = _dot(z_ref[0].astype(BF16), w_ref[...])
    d = x_ref.shape[2]
    o_ref[0] = x_ref[0] + ga_ref[0] * (y[:, :d] * _sigmoid(y[:, d:]))


def _glu_residual(z, w_glu_bf, x, ga):
    bg, lg, d = x.shape
    tb = _tok_block(lg)
    return pl.pallas_call(
        _glu_res_kernel,
        grid=(bg, lg // tb),
        in_specs=[_tok(d, tb), _full(w_glu_bf), _tok(d, tb), _modspec(ga, tb)],
        out_specs=_tok(d, tb),
        out_shape=jax.ShapeDtypeStruct((bg, lg, d), F32),
        compiler_params=_params("parallel", "parallel"),
        name="glu_residual",
    )(z, w_glu_bf, x, ga)


def _cand_pairs(topk):
    return [(a, b) for a in range(topk + 1) for b in range(topk + 1) if (a + 1) * (b + 1) <= topk + 1]


def _top_rows(s, count):
    out = []
    cur = s
    for _ in range(count):
        m = jnp.max(cur, axis=0, keepdims=True)
        out.append(m)
        cur = jnp.where(cur == m, NEG_INF, cur)
    return out


def _peer_route_kernel(x_ref, g_ref, sh_ref, sc_ref, wqh_ref, wql_ref, keys_ref,
                       hb_ref, s1_ref, e1_ref, thr_ref, coef_ref, cand_ref, *, topk):
    h = _norm_mod(x_ref[0], g_ref[...], sh_ref[0], sc_ref[0])
    hb_ref[0] = h.astype(BF16)
    q = _dot3w(h, wqh_ref[...], wql_ref[...])
    n_heads, _, n_keys, dk = keys_ref.shape
    pairs = _cand_pairs(topk)
    cand_ref[...] = jnp.full(cand_ref.shape, NEG_INF, F32)
    for hd in range(n_heads):
        st, tops = [], []
        for s in range(2):
            qs = q[:, (hd * 2 + s) * dk:(hd * 2 + s + 1) * dk]
            sc = _dot3(keys_ref[hd, s], qs, _NT)
            st.append(sc)
            tops.append(_top_rows(sc, topk + 1))
        for r, (a, b) in enumerate(pairs):
            cand_ref[pl.ds(r, 1), :] = tops[0][a] + tops[1][b]
        cand = cand_ref[...]
        best = _top_rows(cand, topk + 1)
        tau = 0.5 * (best[topk - 1] + best[topk])
        mtot = tops[0][0] + tops[1][0]
        z = jnp.sum(jnp.where(cand >= tau, jnp.exp(cand - mtot), 0.0), axis=0, keepdims=True)
        s1_ref[0, hd] = st[1]
        e1_ref[0, hd] = jnp.exp(st[1] - tops[1][0])
        thr_ref[0, hd] = tau - st[0]
        coef_ref[0, hd] = jnp.exp(st[0] - tops[0][0]) / z


def _peer_route(x, g, sh, sc, wqh, wql, keys, topk=PEER_TOPK):
    bg, lg, d = x.shape
    n_heads, _, n_keys, _ = keys.shape
    tb = _tok_block(lg)
    n_cand = -(-len(_cand_pairs(topk)) // 8) * 8
    gate = pl.BlockSpec((1, n_heads, n_keys, tb), lambda b, i: (b, 0, 0, i))
    gate_shape = jax.ShapeDtypeStruct((bg, n_heads, n_keys, lg), F32)
    return pl.pallas_call(
        functools.partial(_peer_route_kernel, topk=topk),
        grid=(bg, lg // tb),
        in_specs=[_tok(d, tb), _full(g), _modspec(sh, tb), _modspec(sc, tb),
                  _full(wqh), _full(wql), _full(keys)],
        out_specs=[_tok(d, tb), gate, gate, gate, gate],
        out_shape=[jax.ShapeDtypeStruct((bg, lg, d), BF16)] + [gate_shape] * 4,
        scratch_shapes=[pltpu.VMEM((n_cand, tb), F32)],
        compiler_params=_params("parallel", "parallel"),
        name="peer_route",
    )(x, g, sh, sc, wqh, wql, keys)


def _peer_dense_kernel(hb_ref, s1_ref, e1_ref, thr_ref, coef_ref, u_ref, vt_ref, x_ref, ga_ref,
                       o_ref, acc_ref, g_ref, *, n_keys):
    c = pl.program_id(2)

    @pl.when(c == 0)
    def _():
        acc_ref[...] = jnp.zeros_like(acc_ref)

    n_heads = s1_ref.shape[1]
    ec = u_ref.shape[0]
    per = ec // n_keys
    act = jax.nn.gelu(_dot(u_ref[...], hb_ref[0], _NT))
    for ii in range(per):
        i = c * per + ii
        gate = None
        for hd in range(n_heads):
            thr = thr_ref[0, hd, pl.ds(i, 1), :]
            coef = coef_ref[0, hd, pl.ds(i, 1), :]
            term = jnp.where(s1_ref[0, hd] >= thr, e1_ref[0, hd] * coef, 0.0)
            gate = term if gate is None else gate + term
        g_ref[ii * n_keys:(ii + 1) * n_keys, :] = (gate * act[ii * n_keys:(ii + 1) * n_keys, :]).astype(BF16)
    acc_ref[...] += _dot(vt_ref[...], g_ref[...])

    @pl.when(c == pl.num_programs(2) - 1)
    def _():
        o_ref[0] = x_ref[0] + ga_ref[0] * acc_ref[...].T


def _peer_dense(hb, gates, u_bf, vt_bf, x, ga):
    bg, lg, d = x.shape
    s1 = gates[0]
    n_heads, n_keys = s1.shape[1], s1.shape[2]
    n_exp = u_bf.shape[0]
    tb = _tok_block(lg)
    ec = min(n_exp, 1024)
    assert n_exp % ec == 0 and ec % n_keys == 0
    gate = pl.BlockSpec((1, n_heads, n_keys, tb), lambda b, i, c: (b, 0, 0, i))
    return pl.pallas_call(
        functools.partial(_peer_dense_kernel, n_keys=n_keys),
        grid=(bg, lg // tb, n_exp // ec),
        in_specs=[_tok(d, tb), gate, gate, gate, gate,
                  pl.BlockSpec((ec, d), lambda b, i, c: (c, 0)),
                  pl.BlockSpec((d, ec), lambda b, i, c: (0, c)),
                  _tok(d, tb), _modspec(ga, tb)],
        out_specs=_tok(d, tb),
        out_shape=jax.ShapeDtypeStruct((bg, lg, d), F32),
        scratch_shapes=[pltpu.VMEM((d, tb), F32), pltpu.VMEM((ec, tb), BF16)],
        compiler_params=_params("parallel", "parallel", "arbitrary"),
        name="peer_dense",
    )(hb, *gates, u_bf, vt_bf, x, ga)


def _peer(x, g, sh, sc, ga, pw):
    hb, *gates = _peer_route(x, g, sh, sc, pw["wqh"], pw["wql"], pw["keys"])
    return _peer_dense(hb, gates, pw["u"], pw["vt"], x, ga)


def _rope_tables(positions, rope_dim, scale=1.0):
    inv = ROPE_THETA ** (-jnp.arange(0, rope_dim, 2, dtype=F32) / rope_dim)
    ang = positions.astype(F32)[:, None] * inv
    cos, sin = jnp.cos(ang), jnp.sin(ang)
    return jnp.concatenate([cos, cos], -1) * scale, jnp.concatenate([sin, sin], -1) * scale


def _rot_cols(w):
    half = w.shape[-1] // 2
    return jnp.concatenate([-w[..., half:], w[..., :half]], axis=-1)


def _kv_kernel(x_ref, g_ref, sh_ref, sc_ref, wdkv_ref, gckv_ref, wkr_ref, wkrr_ref, cos_ref, sin_ref,
               *rest, heads_out):
    h = _norm_mod(x_ref[0], g_ref[...], sh_ref[0], sc_ref[0])
    ckv = _rms(_dot3(h, wdkv_ref[...]), gckv_ref[...])
    kr = _dot3(h, wkr_ref[...]) * cos_ref[...] + _dot3(h, wkrr_ref[...]) * sin_ref[...]
    if heads_out:
        wk_ref, pk_ref, wv_ref, ckv_ref, kr_ref, kcat_ref, v_ref = rest
        cb = ckv.astype(BF16)
        kcat_ref[0] = (_dot(cb, wk_ref[...]) + _dot(kr.astype(BF16), pk_ref[...])).astype(BF16)
        v_ref[0] = _dot(cb, wv_ref[...]).astype(BF16)
    else:
        ckv_ref, kr_ref = rest
    ckv_ref[0] = ckv
    kr_ref[0] = kr


def _kv_proj(x, g, sh, sc, aw, cos, sin, heads_out):
    bg, lg, d = x.shape
    tb = _tok_block(lg)
    c_dim, r_dim = aw["w_dkv"].shape[1], aw["w_kr"].shape[1]
    tab = _full(cos) if cos.shape[0] == 1 else pl.BlockSpec((tb, r_dim), lambda b, i: (i, 0))
    ins = [x, g, sh, sc, aw["w_dkv"], aw["g_ckv"], aw["w_kr"], aw["w_kr_rot"], cos, sin]
    in_specs = [_tok(d, tb), _full(g), _modspec(sh, tb), _modspec(sc, tb), _full(aw["w_dkv"]),
                _full(aw["g_ckv"]), _full(aw["w_kr"]), _full(aw["w_kr_rot"]), tab, tab]
    out_specs = [_tok(c_dim, tb), _tok(r_dim, tb)]
    out_shape = [jax.ShapeDtypeStruct((bg, lg, c_dim), F32), jax.ShapeDtypeStruct((bg, lg, r_dim), F32)]
    if heads_out:
        hw = aw["wk_all"].shape[1]
        ins += [aw["wk_all"], aw["pk"], aw["wv_all"]]
        in_specs += [_full(aw["wk_all"]), _full(aw["pk"]), _full(aw["wv_all"])]
        out_specs += [_tok(hw, tb), _tok(hw, tb)]
        out_shape += [jax.ShapeDtypeStruct((bg, lg, hw), BF16)] * 2
    return pl.pallas_call(
        functools.partial(_kv_kernel, heads_out=heads_out),
        grid=(bg, lg // tb),
        in_specs=in_specs, out_specs=out_specs, out_shape=out_shape,
        compiler_params=_params("parallel", "parallel"),
        name="kv_proj",
    )(*ins)


def _q_kernel(x_ref, g_ref, sh_ref, sc_ref, wdq_ref, gcq_ref, wqa_ref, wqb_ref, cos_ref, sin_ref, q_ref,
              *, n_heads):
    h = _norm_mod(x_ref[0], g_ref[...], sh_ref[0], sc_ref[0])
    cq = _rms(_dot(h.astype(BF16), wdq_ref[...]), gcq_ref[...]).astype(BF16)
    qa = _dot(cq, wqa_ref[...])
    qb = _dot(cq, wqb_ref[...])
    cos = cos_ref[...]
    sin = sin_ref[...]
    for hd in range(n_heads):
        cols = slice(hd * LANES, (hd + 1) * LANES)
        q_ref[0, :, cols] = (qa[:, cols] * cos + qb[:, cols] * sin).astype(q_ref.dtype)


def _q_proj(x, g, sh, sc, lw, cos, sin, n_heads, out_dtype):
    bg, lg, d = x.shape
    tb = _tok_block(lg)
    hw = n_heads * LANES
    tab = _full(cos) if cos.shape[0] == 1 else pl.BlockSpec((tb, LANES), lambda b, i: (i, 0))
    return pl.pallas_call(
        functools.partial(_q_kernel, n_heads=n_heads),
        grid=(bg, lg // tb),
        in_specs=[_tok(d, tb), _full(g), _modspec(sh, tb), _modspec(sc, tb), _full(lw["w_dq"]),
                  _full(lw["g_cq"]), _full(lw["wq_a"]), _full(lw["wq_b"]), tab, tab],
        out_specs=_tok(hw, tb),
        out_shape=jax.ShapeDtypeStruct((bg, lg, hw), out_dtype),
        compiler_params=_params("parallel", "parallel"),
        name="q_proj",
    )(x, g, sh, sc, lw["w_dq"], lw["g_cq"], lw["wq_a"], lw["wq_b"], cos, sin)


def _flash_kernel(q_ref, k_ref, v_ref, o_ref, m_ref, l_ref, acc_ref, *, tq):
    qi = pl.program_id(2)
    q = q_ref[0]
    m_ref[...] = jnp.full(m_ref.shape, NEG_INF, F32)
    l_ref[...] = jnp.zeros_like(l_ref)
    acc_ref[...] = jnp.zeros_like(acc_ref)

    def update(j, masked):
        start = pl.multiple_of(j * tq, tq)
        k = k_ref[0, pl.ds(start, tq), :]
        v = v_ref[0, pl.ds(start, tq), :]
        s = _dot(q, k, _NT)
        if masked:
            row = lax.broadcasted_iota(jnp.int32, s.shape, 0)
            col = lax.broadcasted_iota(jnp.int32, s.shape, 1)
            s = jnp.where(col <= row, s, NEG_INF)
        m_old = m_ref[...]
        m_new = jnp.maximum(m_old, jnp.max(s, axis=-1, keepdims=True))
        corr = jnp.exp(m_old - m_new)
        p = jnp.exp(s - m_new)
        l_ref[...] = l_ref[...] * corr + jnp.sum(p, axis=-1, keepdims=True)
        acc_ref[...] = acc_ref[...] * corr + _dot(p.astype(BF16), v)
        m_ref[...] = m_new

    def body(j, carry):
        update(j, False)
        return carry

    lax.fori_loop(0, qi, body, 0)
    update(qi, True)
    o_ref[0] = (acc_ref[...] / l_ref[...]).astype(o_ref.dtype)


def _flash_attention(q, k, v, n_heads):
    bsz, lg, _ = q.shape
    tq = _tok_block(lg)
    seq = pl.BlockSpec((1, lg, LANES), lambda b, h, i: (b, 0, h))
    blk = pl.BlockSpec((1, tq, LANES), lambda b, h, i: (b, i, h))
    return pl.pallas_call(
        functools.partial(_flash_kernel, tq=tq),
        grid=(bsz, n_heads, lg // tq),
        in_specs=[blk, seq, seq],
        out_specs=blk,
        out_shape=jax.ShapeDtypeStruct(q.shape, BF16),
        scratch_shapes=[pltpu.VMEM((tq, 1), F32), pltpu.VMEM((tq, 1), F32), pltpu.VMEM((tq, LANES), F32)],
        compiler_params=_params("parallel", "parallel", "arbitrary"),
        name="flash_attention",
    )(q, k, v)


def _absorb_kernel(q_ref, wk_ref, qlat_ref, qpe_ref, *, n_heads, nope, rope):
    c_dim = wk_ref.shape[0]
    for hd in range(n_heads):
        qh = q_ref[:, hd * LANES:(hd + 1) * LANES]
        wk = wk_ref[:, hd * LANES:(hd + 1) * LANES]
        qlat_ref[:, hd * c_dim:(hd + 1) * c_dim] = _dot(qh.astype(BF16), wk, _NT)
        qpe_ref[:, hd * rope:(hd + 1) * rope] = qh[:, nope:nope + rope]


def _absorb(q, wk_all, n_heads, nope, rope):
    rows = q.shape[0]
    c_dim = wk_all.shape[0]
    return pl.pallas_call(
        functools.partial(_absorb_kernel, n_heads=n_heads, nope=nope, rope=rope),
        in_specs=[_full(q), _full(wk_all)],
        out_specs=[pl.BlockSpec((rows, n_heads * c_dim), lambda: (0, 0)),
                   pl.BlockSpec((rows, n_heads * rope), lambda: (0, 0))],
        out_shape=[jax.ShapeDtypeStruct((rows, n_heads * c_dim), F32),
                   jax.ShapeDtypeStruct((rows, n_heads * rope), F32)],
        compiler_params=pltpu.CompilerParams(vmem_limit_bytes=VMEM_LIMIT),
        name="q_absorb",
    )(q, wk_all)


def _decode_kernel(pt_ref, qlat_ref, qpe_ref, cnew_ref, rnew_ref, *rest, pages):
    kc_refs = rest[:pages]
    kr_refs = rest[pages:2 * pages]
    o_ref, m_ref, l_ref, acc_ref = rest[2 * pages:]
    c = pl.program_id(1)

    @pl.when(c == 0)
    def _():
        m_ref[...] = jnp.full(m_ref.shape, NEG_INF, F32)
        l_ref[...] = jnp.zeros_like(l_ref)
        acc_ref[...] = jnp.zeros_like(acc_ref)

    ql = qlat_ref[0]
    qp = qpe_ref[0]
    kc = jnp.concatenate([r[0].astype(BF16) for r in kc_refs], axis=0)
    kr = jnp.concatenate([r[0].astype(BF16) for r in kr_refs], axis=0)
    s = _dot(ql.astype(BF16), kc, _NT) + _dot(qp.astype(BF16), kr, _NT)
    m_old = m_ref[...]
    m_new = jnp.maximum(m_old, jnp.max(s, axis=-1, keepdims=True))
    corr = jnp.exp(m_old - m_new)
    p = jnp.exp(s - m_new)
    l_ref[...] = l_ref[...] * corr + jnp.sum(p, axis=-1, keepdims=True)
    acc_ref[...] = acc_ref[...] * corr + _dot(p.astype(BF16), kc)
    m_ref[...] = m_new

    @pl.when(c == pl.num_programs(1) - 1)
    def _():
        cn = cnew_ref[0]
        rn = rnew_ref[0]
        s_new = jnp.sum(ql * cn, axis=-1, keepdims=True) + jnp.sum(qp * rn, axis=-1, keepdims=True)
        m_old = m_ref[...]
        m_new = jnp.maximum(m_old, s_new)
        corr = jnp.exp(m_old - m_new)
        p_new = jnp.exp(s_new - m_new)
        l_new = l_ref[...] * corr + p_new
        o_ref[0] = (acc_ref[...] * corr + p_new * cn) / l_new


def _decode_attention(qlat, qpe, ckv_new, kr_new, cache_c, cache_r, page_table):
    bd, n_heads, c_dim = qlat.shape
    r_dim = qpe.shape[2]
    page = cache_c.shape[1]
    n_pages = page_table.shape[1]
    pages = math.gcd(n_pages, 16)
    per_b3 = lambda w: pl.BlockSpec((1, n_heads, w), lambda b, c, pt: (b, 0, 0))
    new3 = lambda w: pl.BlockSpec((1, 1, w), lambda b, c, pt: (b, 0, 0))

    def page_spec(w, j):
        return pl.BlockSpec((1, page, w), lambda b, c, pt: (pt[b, c * pages + j], 0, 0))

    grid_spec = pltpu.PrefetchScalarGridSpec(
        num_scalar_prefetch=1,
        grid=(bd, n_pages // pages),
        in_specs=[per_b3(c_dim), per_b3(r_dim), new3(c_dim), new3(r_dim)]
        + [page_spec(c_dim, j) for j in range(pages)] + [page_spec(r_dim, j) for j in range(pages)],
        out_specs=per_b3(c_dim),
        scratch_shapes=[pltpu.VMEM((n_heads, 1), F32), pltpu.VMEM((n_heads, 1), F32),
                        pltpu.VMEM((n_heads, c_dim), F32)],
    )
    return pl.pallas_call(
        functools.partial(_decode_kernel, pages=pages),
        grid_spec=grid_spec,
        out_shape=jax.ShapeDtypeStruct((bd, n_heads, c_dim), F32),
        compiler_params=_params("parallel", "arbitrary"),
        name="decode_attention",
    )(page_table, qlat, qpe, ckv_new, kr_new, *([cache_c] * pages), *([cache_r] * pages))


def _upproj_kernel(ol_ref, wv_ref, o_ref, *, n_heads):
    c_dim = wv_ref.shape[0]
    for hd in range(n_heads):
        ol = ol_ref[:, hd * c_dim:(hd + 1) * c_dim].astype(BF16)
        o_ref[0, :, hd * LANES:(hd + 1) * LANES] = _dot(ol, wv_ref[:, hd * LANES:(hd + 1) * LANES]).astype(BF16)


def _upproj(o_lat, wv_all, n_heads):
    rows = o_lat.shape[0]
    hw = wv_all.shape[1]
    return pl.pallas_call(
        functools.partial(_upproj_kernel, n_heads=n_heads),
        in_specs=[_full(o_lat), _full(wv_all)],
        out_specs=pl.BlockSpec((1, rows, hw), lambda: (0, 0, 0)),
        out_shape=jax.ShapeDtypeStruct((1, rows, hw), BF16),
        compiler_params=pltpu.CompilerParams(vmem_limit_bytes=VMEM_LIMIT),
        name="v_upproj",
    )(o_lat, wv_all)


def _out_res_kernel(o_ref, w_ref, x_ref, ga_ref, y_ref):
    y_ref[0] = x_ref[0] + ga_ref[0] * _dot(o_ref[0], w_ref[...])


def _out_residual(o, w_o_pad, x, ga):
    bg, lg, d = x.shape
    tb = _tok_block(lg)
    return pl.pallas_call(
        _out_res_kernel,
        grid=(bg, lg // tb),
        in_specs=[_tok(o.shape[2], tb), _full(w_o_pad), _tok(d, tb), _modspec(ga, tb)],
        out_specs=_tok(d, tb),
        out_shape=jax.ShapeDtypeStruct((bg, lg, d), F32),
        compiler_params=_params("parallel", "parallel"),
        name="out_residual",
    )(o, w_o_pad, x, ga)


def _final_kernel(x_ref, g_ref, sh_ref, sc_ref, y_ref):
    y_ref[0] = _norm_mod(x_ref[0], g_ref[...], sh_ref[0], sc_ref[0])


def _final_norm(x, g, sh, sc):
    bg, lg, d = x.shape
    tb = _tok_block(lg)
    return pl.pallas_call(
        _final_kernel,
        grid=(bg, lg // tb),
        in_specs=[_tok(d, tb), _full(g), _modspec(sh, tb), _modspec(sc, tb)],
        out_specs=_tok(d, tb),
        out_shape=jax.ShapeDtypeStruct((bg, lg, d), F32),
        compiler_params=_params("parallel", "parallel"),
        name="final_norm",
    )(x, g, sh, sc)


def _pad_heads(w, offset=0):
    pad = [(0, 0)] * (w.ndim - 1) + [(offset, LANES - offset - w.shape[-1])]
    wp = jnp.pad(w, pad)
    return wp.reshape(wp.shape[:-2] + (wp.shape[-2] * LANES,))


def _attn_weights(w_dkv, g_ckv, w_kr, w_ukv, nope, n_heads):
    r_dim = w_kr.shape[1]
    pk = jnp.tile(jnp.pad(jnp.eye(r_dim, dtype=F32), ((0, 0), (nope, LANES - nope - r_dim))), (1, n_heads))
    return dict(w_dkv=w_dkv, g_ckv=g_ckv.reshape(1, -1), w_kr=w_kr, w_kr_rot=_rot_cols(w_kr),
                wk_all=_pad_heads(w_ukv[..., :nope]).astype(BF16),
                wv_all=_pad_heads(w_ukv[..., nope:]).astype(BF16),
                pk=pk.astype(BF16))


def _query_weights(w_dq, g_cq, w_uq, w_o, nope, rope, n_heads):
    q_lora = w_uq.shape[0]
    wq = w_uq.reshape(q_lora, n_heads, nope + rope)
    w_pe = wq[..., nope:]
    wq_a = _pad_heads(wq)
    wq_b = _pad_heads(_rot_cols(w_pe), offset=nope)
    v_head = w_o.shape[0] // n_heads
    w_o_pad = jnp.pad(w_o.reshape(n_heads, v_head, -1), ((0, 0), (0, LANES - v_head), (0, 0)))
    return dict(w_dq=w_dq.astype(BF16), g_cq=g_cq.reshape(1, -1), wq_a=wq_a.astype(BF16),
                wq_b=wq_b.astype(BF16), w_o_pad=w_o_pad.reshape(n_heads * LANES, -1).astype(BF16))


def _query_tables(positions, nope, rope, scale):
    cos, sin = _rope_tables(positions, rope, scale)
    n = positions.shape[0]
    tail = LANES - nope - rope
    cos_t = jnp.concatenate([jnp.full((n, nope), scale, F32), cos, jnp.zeros((n, tail), F32)], -1)
    sin_t = jnp.concatenate([jnp.zeros((n, nope), F32), sin, jnp.zeros((n, tail), F32)], -1)
    return cos_t, sin_t


def kernel(x_prompt, x_sample, state_ssm_re, state_ssm_im, cache_kv_latent, cache_k_rope, page_table,
           c_prompt, c_sample,
           w_mod, b_mod, g_norm,
           ssm_a_re, ssm_a_im, ssm_log_dt, ssm_b_re, ssm_b_im, ssm_c_re, ssm_c_im, ssm_d, ssm_w_glu,
           w_mod_kv, b_mod_kv, g_kv_norm, w_dkv, g_ckv, w_kr, w_ukv,
           w_dq, g_cq, w_uq, w_o,
           peer_w_q, peer_keys, peer_u, peer_v,
           w_mod_final, b_mod_final, g_final):
    bsz, seq, d = x_prompt.shape
    bd, dec_seq, _ = x_sample.shape
    assert dec_seq == 1
    depth = w_mod.shape[0]
    n_a = ssm_a_re.shape[0]
    n_groups, n_state = ssm_a_re.shape[1], ssm_a_re.shape[2]
    n_heads = w_ukv.shape[1]
    rope = w_kr.shape[1]
    nope = w_uq.shape[2] // n_heads - rope
    page = cache_kv_latent.shape[1]
    past_len = page_table.shape[1] * page
    sm_scale = 1.0 / math.sqrt(nope + rope)

    c_all = jnp.concatenate([c_prompt, c_sample], axis=0)

    def grouped(m, n):
        parts = jnp.split(m, n, axis=-1)
        return [q[:bsz][:, None, :] for q in parts], [q[bsz:][None] for q in parts]

    mods = [grouped(_mod_linear(c_all, w_mod[l], b_mod[l]), 6) for l in range(depth)]
    mod_kv = grouped(_mod_linear(c_all, w_mod_kv, b_mod_kv), 2)
    mod_f = grouped(_mod_linear(c_all, w_mod_final, b_mod_final), 2)

    s5 = [_s5_prep(ssm_a_re[l], ssm_a_im[l], ssm_log_dt[l], ssm_b_re[l], ssm_b_im[l],
                   ssm_c_re[l], ssm_c_im[l]) for l in range(n_a)]
    glu_w = [ssm_w_glu[l].astype(BF16) for l in range(n_a)]
    peer_w = []
    for l in range(depth):
        wqh, wql = _hi_lo(peer_w_q[l])
        peer_w.append(dict(wqh=wqh, wql=wql, keys=peer_keys[l], u=peer_u[l].astype(BF16),
                           vt=peer_v[l].T.astype(BF16)))
    aw = _attn_weights(w_dkv, g_ckv, w_kr, w_ukv, nope, n_heads)
    qw = [_query_weights(w_dq[j], g_cq[j], w_uq[j], w_o[j], nope, rope, n_heads) for j in range(depth - n_a)]

    pos_p = jnp.arange(seq, dtype=jnp.int32)
    pos_s = past_len + jnp.arange(dec_seq, dtype=jnp.int32)
    tables = []
    for pos in (pos_p, pos_s):
        tables.append(dict(k=_rope_tables(pos, rope), q=_query_tables(pos, nope, rope, sm_scale)))

    def gvec(v):
        return v.reshape(1, -1)

    def run(x, grp, h0_re, h0_im):
        is_prompt = grp == 0
        tab = tables[grp]
        ssm_re, ssm_im = [], []
        ckv = kr = kcat = vpad = None
        for layer in range(depth):
            sh1, sc1, ga1, sh2, sc2, ga2 = mods[layer][grp]
            if layer == n_a:
                mk = mod_kv[grp]
                outs = _kv_proj(x, gvec(g_kv_norm), mk[0], mk[1], aw, *tab["k"], heads_out=is_prompt)
                if is_prompt:
                    ckv, kr, kcat, vpad = outs
                else:
                    ckv, kr = outs
            if layer < n_a:
                if is_prompt:
                    z, hr, hi = _s5_prompt(x, gvec(g_norm[layer, 0]), sh1, sc1, s5[layer], gvec(ssm_d[layer]))
                    hr, hi = hr[:, 0], hi[:, 0]
                else:
                    z, hr, hi = _s5_step(x, gvec(g_norm[layer, 0]), sh1, sc1, h0_re[layer], h0_im[layer],
                                         s5[layer], gvec(ssm_d[layer]))
                ssm_re.append(hr.reshape(-1, n_groups, n_state))
                ssm_im.append(hi.reshape(-1, n_groups, n_state))
                x = _glu_residual(z, glu_w[layer], x, ga1)
            else:
                lw = qw[layer - n_a]
                if is_prompt:
                    q = _q_proj(x, gvec(g_norm[layer, 0]), sh1, sc1, lw, *tab["q"], n_heads, BF16)
                    o = _flash_attention(q, kcat, vpad, n_heads)
                else:
                    q = _q_proj(x, gvec(g_norm[layer, 0]), sh1, sc1, lw, *tab["q"], n_heads, F32)
                    qlat, qpe = _absorb(q[0], aw["wk_all"], n_heads, nope, rope)
                    rows = qlat.shape[0]
                    o_lat = _decode_attention(qlat.reshape(rows, n_heads, -1), qpe.reshape(rows, n_heads, -1),
                                              ckv.reshape(rows, 1, -1), kr.reshape(rows, 1, -1),
                                              cache_kv_latent, cache_k_rope, page_table)
                    o = _upproj(o_lat.reshape(rows, -1), aw["wv_all"], n_heads)
                x = _out_residual(o, lw["w_o_pad"], x, ga1)
            x = _peer(x, gvec(g_norm[layer, 1]), sh2, sc2, ga2, peer_w[layer])
        shf, scf = mod_f[grp]
        y = _final_norm(x, gvec(g_final), shf, scf)
        return y, jnp.stack(ssm_re), jnp.stack(ssm_im), ckv, kr

    y_p, re_p, im_p, ckv_p, kr_p = run(x_prompt, 0, None, None)
    gn = n_groups * n_state
    h0_re = state_ssm_re.reshape(n_a, bd, gn)
    h0_im = state_ssm_im.reshape(n_a, bd, gn)
    y_s, re_s, im_s, ckv_s, kr_s = run(x_sample.reshape(1, bd, d), 1, h0_re, h0_im)
    return (y_p, y_s.reshape(bd, dec_seq, d), re_p, im_p, ckv_p, kr_p,
            re_s, im_s, ckv_s.reshape(bd, dec_seq, -1), kr_s.reshape(bd, dec_seq, -1))
```

```python
import functools
import math

import jax
import jax.numpy as jnp
from jax import lax
from jax.experimental import pallas as pl
from jax.experimental.pallas import tpu as pltpu

F32 = jnp.float32
BF16 = jnp.bfloat16
EPS = 1e-6
ROPE_THETA = 10000.0
PEER_TOPK = 16
NEG_INF = -1e30
LANES = 128
PEER_GATE_W = 512
SSM_GROUPS_PER_BLOCK = 16
VMEM_LIMIT = 56 * 1024 * 1024

_NN = (((1,), (0,)), ((), ()))
_NT = (((1,), (1,)), ((), ()))


def _dot(a, b, dims=_NN):
    return lax.dot_general(a, b, dims, preferred_element_type=F32)


def _split(a):
    hi = a.astype(BF16)
    lo = (a - hi.astype(F32)).astype(BF16)
    return hi, lo


def _dot3(a, b, dims=_NN):
    ah, al = _split(a)
    bh, bl = _split(b)
    return _dot(ah, bh, dims) + (_dot(ah, bl, dims) + _dot(al, bh, dims))


def _dot3w(a, wh, wl, dims=_NN):
    ah, al = _split(a)
    return _dot(ah, wh, dims) + (_dot(ah, wl, dims) + _dot(al, wh, dims))


def _sigmoid(x):
    return 1.0 / (1.0 + jnp.exp(-x))


def _rms(x, g):
    return x * lax.rsqrt(jnp.mean(x * x, axis=-1, keepdims=True) + EPS) * g


def _norm_mod(x, g, sh, sc):
    return _rms(x, g) * (1.0 + sc) + sh


def _params(*sem, flags=None):
    return pltpu.CompilerParams(dimension_semantics=sem, vmem_limit_bytes=VMEM_LIMIT, flags=flags)


def _full(a):
    nd = a.ndim
    return pl.BlockSpec(a.shape, lambda *_: (0,) * nd)


def _tok(width, tb):
    return pl.BlockSpec((1, tb, width), lambda b, i, *_: (b, i, 0))


def _modspec(m, tb):
    if m.shape[1] == 1:
        return pl.BlockSpec((1, 1, m.shape[2]), lambda b, i, *_: (b, 0, 0))
    return pl.BlockSpec((1, tb, m.shape[2]), lambda b, i, *_: (b, i, 0))


def _tok_block(lg, cap=512):
    tb = min(lg, cap)
    assert lg % tb == 0
    return tb


def _hi_lo(w):
    hi = lax.bitcast_convert_type(lax.bitcast_convert_type(w, jnp.uint32) & jnp.uint32(0xFFFF0000), F32)
    return hi.astype(BF16), (w - hi).astype(BF16)


def _mod_kernel(c_ref, wh_ref, wl_ref, b_ref, o_ref):
    c = c_ref[...]
    o_ref[...] = _dot3w(c * _sigmoid(c), wh_ref[...], wl_ref[...]) + b_ref[...]


def _mod_linear(c, w, b):
    bc, d = c.shape
    n = w.shape[1]
    tn = min(n, 1024)
    assert n % tn == 0
    wh, wl = _hi_lo(w)
    return pl.pallas_call(
        _mod_kernel,
        grid=(n // tn,),
        in_specs=[pl.BlockSpec((bc, d), lambda j: (0, 0)),
                  pl.BlockSpec((d, tn), lambda j: (0, j)),
                  pl.BlockSpec((d, tn), lambda j: (0, j)),
                  pl.BlockSpec((1, tn), lambda j: (0, j))],
        out_specs=pl.BlockSpec((bc, tn), lambda j: (0, j)),
        out_shape=jax.ShapeDtypeStruct((bc, n), F32),
        compiler_params=_params("parallel"),
        name="mod_linear",
    )(c, wh, wl, b.reshape(1, n))


def _s5_prep(a_re, a_im, log_dt, b_re, b_im, c_re, c_im):
    g, n, p = b_re.shape
    gb = min(g, SSM_GROUPS_PER_BLOCK)
    assert g % gb == 0
    k = g // gb
    dt = jnp.exp(log_dt.astype(F32))[:, None]
    a_re = a_re.astype(F32)
    a_im = a_im.astype(F32)
    er = jnp.exp(a_re * dt)
    lr = er * jnp.cos(a_im * dt)
    li = er * jnp.sin(a_im * dt)
    den = a_re * a_re + a_im * a_im
    kr = ((lr - 1.0) * a_re + li * a_im) / den
    ki = (li * a_re - (lr - 1.0) * a_im) / den
    bbr = kr[..., None] * b_re - ki[..., None] * b_im
    bbi = kr[..., None] * b_im + ki[..., None] * b_re
    eye = jnp.eye(gb, dtype=F32)

    def blk_in(b):
        bt = b.reshape(k, gb, n, p).transpose(0, 1, 3, 2)
        return (bt[:, :, :, None, :] * eye[None, :, None, :, None]).reshape(k, gb * p, gb * n)

    def blk_out(c):
        ct = c.reshape(k, gb, p, n).transpose(0, 1, 3, 2)
        return (ct[:, :, :, None, :] * eye[None, :, None, :, None]).reshape(k, gb * n, gb * p)

    b_blk = jnp.concatenate([blk_in(bbr), blk_in(bbi)], axis=-1)
    c_blk = jnp.concatenate([blk_out(c_re.astype(F32)), -blk_out(c_im.astype(F32))], axis=1)
    bh, bl = _hi_lo(b_blk)
    return dict(bh=bh, bl=bl, c=c_blk.astype(BF16), lam_re=lr.reshape(1, g * n), lam_im=li.reshape(1, g * n))


def _s5_input(h, bh_ref, bl_ref, k, w):
    hs = h[:, k * w:(k + 1) * w]
    return _dot3w(hs, bh_ref[k], bl_ref[k])


def _s5_prompt_kernel(x_ref, g_ref, sh_ref, sc_ref, bh_ref, bl_ref, c_ref, lre_ref, lim_ref, d_ref,
                      z_ref, sre_ref, sim_ref, s_re, s_im, st_re, st_im, h_ref, hh_ref, hl_ref,
                      *, nblk, tc, pitch, kper):
    j = pl.program_id(0)

    @pl.when(j == 0)
    def _():
        st_re[...] = jnp.zeros_like(st_re)
        st_im[...] = jnp.zeros_like(st_im)

    bsz, _, d = x_ref.shape
    w = d // nblk
    nw = lre_ref.shape[2]
    n_slabs = nw // LANES
    rows = kper * bsz
    for b in range(bsz):
        h = _norm_mod(x_ref[b], g_ref[...], sh_ref[b], sc_ref[b])
        h_ref[b] = h
        hh, hl = _split(h)
        hh_ref[b * tc:(b + 1) * tc, :] = hh
        hl_ref[b * tc:(b + 1) * tc, :] = hl

    for pair in range(nblk // kper):
        for kk in range(kper):
            k = pair * kper + kk
            hh = hh_ref[:, k * w:(k + 1) * w]
            hl = hl_ref[:, k * w:(k + 1) * w]
            bu = _dot(hh, bh_ref[k]) + (_dot(hh, bl_ref[k]) + _dot(hl, bh_ref[k]))
            for b in range(bsz):
                r0 = (kk * bsz + b) * pitch
                for sl in range(n_slabs):
                    s_re[sl, r0:r0 + tc, :] = bu[b * tc:(b + 1) * tc, sl * LANES:(sl + 1) * LANES]
                    s_im[sl, r0:r0 + tc, :] = bu[b * tc:(b + 1) * tc, nw + sl * LANES:nw + (sl + 1) * LANES]
        lr = lre_ref[pair]
        li = lim_ref[pair]

        def step(t, carry):
            sr, si = carry
            idx = pl.ds(t, rows, stride=pitch)
            br = jnp.concatenate([s_re[sl, idx, :] for sl in range(n_slabs)], axis=1)
            bi = jnp.concatenate([s_im[sl, idx, :] for sl in range(n_slabs)], axis=1)
            nr = lr * sr - li * si + br
            ni = lr * si + li * sr + bi
            for sl in range(n_slabs):
                s_re[sl, idx, :] = nr[:, sl * LANES:(sl + 1) * LANES]
                s_im[sl, idx, :] = ni[:, sl * LANES:(sl + 1) * LANES]
            return nr, ni

        sr, si = lax.fori_loop(0, tc, step, (st_re[pair], st_im[pair]))
        st_re[pair] = sr
        st_im[pair] = si
        for kk in range(kper):
            k = pair * kper + kk
            for b in range(bsz):
                r0 = (kk * bsz + b) * pitch
                hre = jnp.concatenate([s_re[sl, r0:r0 + tc, :] for sl in range(n_slabs)], axis=1)
                him = jnp.concatenate([s_im[sl, r0:r0 + tc, :] for sl in range(n_slabs)], axis=1)
                y = _dot(hre.astype(BF16), c_ref[k, :nw, :]) + _dot(him.astype(BF16), c_ref[k, nw:, :])
                hk = h_ref[b, :, k * w:(k + 1) * w]
                z_ref[b, :, k * w:(k + 1) * w] = jax.nn.gelu(y + d_ref[:, k * w:(k + 1) * w] * hk)
    sre_ref[...] = st_re[...]
    sim_ref[...] = st_im[...]


def _s5_prompt(x, g, sh, sc, prm, d_skip):
    bsz, lg, d = x.shape
    nblk = prm["bh"].shape[0]
    gn = prm["lam_re"].shape[1]
    nw = gn // nblk
    sublanes = 8
    assert sublanes % bsz == 0 and nblk % (sublanes // bsz) == 0 and nw % LANES == 0
    kper = sublanes // bsz
    npairs = nblk // kper
    tc = _tok_block(lg, 128)
    pitch = tc + 8 if (tc // 8) % 2 == 0 else tc
    rows = kper * bsz

    def pack_lam(v):
        return jnp.repeat(v.reshape(npairs, kper, nw), bsz, axis=1)

    lre, lim = pack_lam(prm["lam_re"]), pack_lam(prm["lam_im"])
    kern = functools.partial(_s5_prompt_kernel, nblk=nblk, tc=tc, pitch=pitch, kper=kper)
    blk = pl.BlockSpec((bsz, tc, d), lambda j: (0, j, 0))
    st_spec = pl.BlockSpec((npairs, rows, nw), lambda j: (0, 0, 0))
    st_shape = jax.ShapeDtypeStruct((npairs, rows, nw), F32)
    z, sre, sim = pl.pallas_call(
        kern,
        grid=(lg // tc,),
        in_specs=[blk, _full(g), _full(sh), _full(sc),
                  _full(prm["bh"]), _full(prm["bl"]), _full(prm["c"]), _full(lre), _full(lim), _full(d_skip)],
        out_specs=[blk, st_spec, st_spec],
        out_shape=[jax.ShapeDtypeStruct((bsz, lg, d), F32), st_shape, st_shape],
        scratch_shapes=[pltpu.VMEM((nw // LANES, rows * pitch, LANES), F32),
                        pltpu.VMEM((nw // LANES, rows * pitch, LANES), F32),
                        pltpu.VMEM((npairs, rows, nw), F32), pltpu.VMEM((npairs, rows, nw), F32),
                        pltpu.VMEM((bsz, tc, d), F32), pltpu.VMEM((bsz * tc, d), BF16),
                        pltpu.VMEM((bsz * tc, d), BF16)],
        compiler_params=_params("arbitrary"),
        name="s5_prompt",
    )(x, g, sh, sc, prm["bh"], prm["bl"], prm["c"], lre, lim, d_skip)

    def unpack(s):
        return s.reshape(npairs, kper, bsz, nw).transpose(2, 0, 1, 3).reshape(bsz, gn)

    return z, unpack(sre), unpack(sim)


def _s5_step_kernel(x_ref, g_ref, sh_ref, sc_ref, h0re_ref, h0im_ref, bh_ref, bl_ref, c_ref,
                    lre_ref, lim_ref, d_ref, z_ref, sre_ref, sim_ref, *, nblk):
    h = _norm_mod(x_ref[0], g_ref[...], sh_ref[0], sc_ref[0])
    w = h.shape[1] // nblk
    nw = lre_ref.shape[1] // nblk
    for k in range(nblk):
        cols = slice(k * nw, (k + 1) * nw)
        bu = _s5_input(h, bh_ref, bl_ref, k, w)
        lr = lre_ref[:, cols]
        li = lim_ref[:, cols]
        pr = h0re_ref[:, cols]
        pi = h0im_ref[:, cols]
        nr = lr * pr - li * pi + bu[:, :nw]
        ni = lr * pi + li * pr + bu[:, nw:]
        sre_ref[:, cols] = nr
        sim_ref[:, cols] = ni
        y = _dot(nr.astype(BF16), c_ref[k, :nw, :]) + _dot(ni.astype(BF16), c_ref[k, nw:, :])
        hk = h[:, k * w:(k + 1) * w]
        z_ref[0, :, k * w:(k + 1) * w] = jax.nn.gelu(y + d_ref[:, k * w:(k + 1) * w] * hk)


def _s5_step(x, g, sh, sc, h0_re, h0_im, prm, d_skip):
    _, rows, d = x.shape
    nblk = prm["bh"].shape[0]
    gn = prm["lam_re"].shape[1]
    tb = _tok_block(rows, 128)
    st_spec = pl.BlockSpec((tb, gn), lambda b, i: (i, 0))
    return pl.pallas_call(
        functools.partial(_s5_step_kernel, nblk=nblk),
        grid=(1, rows // tb),
        in_specs=[_tok(d, tb), _full(g), _modspec(sh, tb), _modspec(sc, tb), st_spec, st_spec,
                  _full(prm["bh"]), _full(prm["bl"]), _full(prm["c"]),
                  _full(prm["lam_re"]), _full(prm["lam_im"]), _full(d_skip)],
        out_specs=[_tok(d, tb), st_spec, st_spec],
        out_shape=[jax.ShapeDtypeStruct((1, rows, d), F32),
                   jax.ShapeDtypeStruct((rows, gn), F32),
                   jax.ShapeDtypeStruct((rows, gn), F32)],
        compiler_params=_params("parallel", "parallel"),
        name="s5_step",
    )(x, g, sh, sc, h0_re, h0_im, prm["bh"], prm["bl"], prm["c"], prm["lam_re"], prm["lam_im"], d_skip)


def _glu_res_kernel(z_ref, w_ref, x_ref, ga_ref, o_ref):
    y = _dot(z_ref[0].astype(BF16), w_ref[...])
    d = x_ref.shape[2]
    o_ref[0] = x_ref[0] + ga_ref[0] * (y[:, :d] * _sigmoid(y[:, d:]))


def _glu_residual(z, w_glu_bf, x, ga):
    bg, lg, d = x.shape
    tb = _tok_block(lg)
    return pl.pallas_call(
        _glu_res_kernel,
        grid=(bg, lg // tb),
        in_specs=[_tok(d, tb), _full(w_glu_bf), _tok(d, tb), _modspec(ga, tb)],
        out_specs=_tok(d, tb),
        out_shape=jax.ShapeDtypeStruct((bg, lg, d), F32),
        compiler_params=_params("parallel", "parallel"),
        name="glu_residual",
    )(z, w_glu_bf, x, ga)


def _cand_pairs(topk):
    return [(a, b) for a in range(topk + 1) for b in range(topk + 1) if (a + 1) * (b + 1) <= topk + 1]


def _top_rows(s, count, rank_count=0):
    out = []
    rank = jnp.zeros(s.shape, F32) if rank_count else None
    for r in range(count):
        if r == 0:
            cur = s
        else:
            below = s < out[-1]
            cur = jnp.where(below, s, NEG_INF)
            if r <= rank_count:
                rank = jnp.where(below, float(r), rank)
        out.append(jnp.max(cur, axis=0, keepdims=True))
    return out, rank


def _peer_route_kernel(x_ref, g_ref, sh_ref, sc_ref, wqh_ref, wql_ref, keys_ref,
                       hb_ref, r1_ref, e1_ref, nb_ref, coef_ref, cand_ref, *, topk):
    h = _norm_mod(x_ref[0], g_ref[...], sh_ref[0], sc_ref[0])
    hb_ref[0] = h.astype(BF16)
    q = _dot3w(h, wqh_ref[...], wql_ref[...])
    n_heads, _, n_keys, dk = keys_ref.shape
    pairs = _cand_pairs(topk)
    cand_ref[...] = jnp.full(cand_ref.shape, NEG_INF, F32)
    for hd in range(n_heads):
        st, tops, rank1 = [], [], None
        for s in range(2):
            qs = q[:, (hd * 2 + s) * dk:(hd * 2 + s + 1) * dk]
            sc = _dot3(keys_ref[hd, s], qs, _NT)
            st.append(sc)
            top, rank = _top_rows(sc, topk + 1, rank_count=topk * s)
            tops.append(top)
            rank1 = rank
        for r, (a, b) in enumerate(pairs):
            cand_ref[pl.ds(r, 1), :] = tops[0][a] + tops[1][b]
        cand = cand_ref[...]
        best, _ = _top_rows(cand, topk + 1)
        tau = 0.5 * (best[topk - 1] + best[topk])
        mtot = tops[0][0] + tops[1][0]
        z = jnp.sum(jnp.where(cand >= tau, jnp.exp(cand - mtot), 0.0), axis=0, keepdims=True)
        nb = jnp.zeros(st[0].shape, F32)
        for b in range(topk):
            nb = jnp.where(st[0] >= tau - tops[1][b], float(b + 1), nb)
        outs = ((r1_ref, rank1.astype(BF16)), (e1_ref, jnp.exp(st[1] - tops[1][0]).astype(BF16)),
                (nb_ref, nb), (coef_ref, jnp.exp(st[0] - tops[0][0]) / z))
        gw = r1_ref.shape[4]
        for ref, val in outs:
            for lt in range(val.shape[1] // gw):
                tile = val[:, lt * gw:(lt + 1) * gw]
                ref[0, hd, lt] = pltpu.bitcast(tile, jnp.uint32) if val.dtype == BF16 else tile


def _peer_route(x, g, sh, sc, wqh, wql, keys, topk=PEER_TOPK):
    bg, lg, d = x.shape
    n_heads, _, n_keys, _ = keys.shape
    tb = _tok_block(lg)
    n_cand = -(-len(_cand_pairs(topk)) // 8) * 8
    gw = min(tb, PEER_GATE_W)

    def gate(rows):
        return pl.BlockSpec((1, n_heads, tb // gw, rows, gw), lambda b, i: (b, 0, i, 0, 0))

    def gate_shape(rows, dt):
        return jax.ShapeDtypeStruct((bg, n_heads, lg // gw, rows, gw), dt)

    packed = n_keys // 2
    gate_shapes = [gate_shape(packed, jnp.uint32), gate_shape(packed, jnp.uint32),
                   gate_shape(n_keys, F32), gate_shape(n_keys, F32)]
    return pl.pallas_call(
        functools.partial(_peer_route_kernel, topk=topk),
        grid=(bg, lg // tb),
        in_specs=[_tok(d, tb), _full(g), _modspec(sh, tb), _modspec(sc, tb),
                  _full(wqh), _full(wql), _full(keys)],
        out_specs=[_tok(d, tb), gate(packed), gate(packed), gate(n_keys), gate(n_keys)],
        out_shape=[jax.ShapeDtypeStruct((bg, lg, d), BF16)] + gate_shapes,
        scratch_shapes=[pltpu.VMEM((n_cand, tb), F32)],
        compiler_params=_params("parallel", "parallel"),
        name="peer_route",
    )(x, g, sh, sc, wqh, wql, keys)


def _peer_dense_kernel(hb_ref, r1_ref, e1_ref, nb_ref, coef_ref, u_ref, vt_ref, x_ref, ga_ref,
                       o_ref, acc_ref, *gate_refs, n_keys):
    c = pl.program_id(2)
    n_steps = pl.num_programs(2)
    n_heads = r1_ref.shape[1]
    gw = r1_ref.shape[4]
    tb = hb_ref.shape[1]
    n_sub = len(gate_refs)
    ec = u_ref.shape[0] // n_sub
    per = ec // n_keys

    def build_gate(chunk, dst):
        for lt in range(tb // gw):
            for ii in range(per):
                i = chunk * per + ii
                gate = None
                for hd in range(n_heads):
                    nb = nb_ref[0, hd, lt, pl.ds(i, 1), :].astype(BF16)
                    coef = coef_ref[0, hd, lt, pl.ds(i, 1), :].astype(BF16)
                    r1 = pltpu.bitcast(r1_ref[0, hd, lt], BF16)
                    e1 = pltpu.bitcast(e1_ref[0, hd, lt], BF16)
                    term = jnp.where(r1 < nb, e1 * coef, jnp.zeros((), BF16))
                    gate = term if gate is None else gate + term
                dst[ii * n_keys:(ii + 1) * n_keys, lt * gw:(lt + 1) * gw] = gate

    @pl.when(c == 0)
    def _():
        acc_ref[...] = jnp.zeros_like(acc_ref)

    for s in range(n_sub):
        rows = slice(s * ec, (s + 1) * ec)
        build_gate(c * n_sub + s, gate_refs[s])
        act = jax.nn.gelu(_dot(u_ref[rows, :], hb_ref[0], _NT).astype(BF16))
        acc_ref[...] += _dot(vt_ref[:, rows], act * gate_refs[s][...])

    @pl.when(c == n_steps - 1)
    def _():
        o_ref[0] = x_ref[0] + ga_ref[0] * acc_ref[...].T


def _peer_dense(hb, gates, u_bf, vt_bf, x, ga):
    bg, lg, d = x.shape
    n_heads, n_keys = gates[2].shape[1], gates[2].shape[3]
    n_exp = u_bf.shape[0]
    tb = _tok_block(lg)
    sub = min(n_exp, 1024)
    n_sub = 2 if n_exp % (2 * sub) == 0 else 1
    ec = n_sub * sub
    assert n_exp % ec == 0 and sub % n_keys == 0

    def gate(a):
        gw = a.shape[4]
        return pl.BlockSpec((1, n_heads, tb // gw, a.shape[3], gw), lambda b, i, c: (b, 0, i, 0, 0))

    return pl.pallas_call(
        functools.partial(_peer_dense_kernel, n_keys=n_keys),
        grid=(bg, lg // tb, n_exp // ec),
        in_specs=[_tok(d, tb)] + [gate(a) for a in gates] + [
                  pl.BlockSpec((ec, d), lambda b, i, c: (c, 0)),
                  pl.BlockSpec((d, ec), lambda b, i, c: (0, c)),
                  _tok(d, tb), _modspec(ga, tb)],
        out_specs=_tok(d, tb),
        out_shape=jax.ShapeDtypeStruct((bg, lg, d), F32),
        scratch_shapes=[pltpu.VMEM((d, tb), F32)] + [pltpu.VMEM((sub, tb), BF16)] * n_sub,
        compiler_params=_params("parallel", "parallel", "arbitrary"),
        name="peer_dense",
    )(hb, *gates, u_bf, vt_bf, x, ga)


def _peer(x, g, sh, sc, ga, pw):
    hb, *gates = _peer_route(x, g, sh, sc, pw["wqh"], pw["wql"], pw["keys"])
    return _peer_dense(hb, gates, pw["u"], pw["vt"], x, ga)


def _rope_tables(positions, rope_dim, scale=1.0):
    inv = ROPE_THETA ** (-jnp.arange(0, rope_dim, 2, dtype=F32) / rope_dim)
    ang = positions.astype(F32)[:, None] * inv
    cos, sin = jnp.cos(ang), jnp.sin(ang)
    return jnp.concatenate([cos, cos], -1) * scale, jnp.concatenate([sin, sin], -1) * scale


def _rot_cols(w):
    half = w.shape[-1] // 2
    return jnp.concatenate([-w[..., half:], w[..., :half]], axis=-1)


def _kv_kernel(x_ref, g_ref, sh_ref, sc_ref, wdkv_ref, gckv_ref, wkr_ref, wkrr_ref, cos_ref, sin_ref,
               *rest, heads_out):
    h = _norm_mod(x_ref[0], g_ref[...], sh_ref[0], sc_ref[0])
    ckv = _rms(_dot3(h, wdkv_ref[...]), gckv_ref[...])
    kr = _dot3(h, wkr_ref[...]) * cos_ref[...] + _dot3(h, wkrr_ref[...]) * sin_ref[...]
    if heads_out:
        wk_ref, pk_ref, wv_ref, ckv_ref, kr_ref, kcat_ref, v_ref = rest
        cb = ckv.astype(BF16)
        kcat_ref[0] = (_dot(cb, wk_ref[...]) + _dot(kr.astype(BF16), pk_ref[...])).astype(BF16)
        v_ref[0] = _dot(cb, wv_ref[...]).astype(BF16)
    else:
        ckv_ref, kr_ref = rest
    ckv_ref[0] = ckv
    kr_ref[0] = kr


def _kv_proj(x, g, sh, sc, aw, cos, sin, heads_out):
    bg, lg, d = x.shape
    tb = _tok_block(lg)
    c_dim, r_dim = aw["w_dkv"].shape[1], aw["w_kr"].shape[1]
    tab = _full(cos) if cos.shape[0] == 1 else pl.BlockSpec((tb, r_dim), lambda b, i: (i, 0))
    ins = [x, g, sh, sc, aw["w_dkv"], aw["g_ckv"], aw["w_kr"], aw["w_kr_rot"], cos, sin]
    in_specs = [_tok(d, tb), _full(g), _modspec(sh, tb), _modspec(sc, tb), _full(aw["w_dkv"]),
                _full(aw["g_ckv"]), _full(aw["w_kr"]), _full(aw["w_kr_rot"]), tab, tab]
    out_specs = [_tok(c_dim, tb), _tok(r_dim, tb)]
    out_shape = [jax.ShapeDtypeStruct((bg, lg, c_dim), F32), jax.ShapeDtypeStruct((bg, lg, r_dim), F32)]
    if heads_out:
        hw = aw["wk_all"].shape[1]
        ins += [aw["wk_all"], aw["pk"], aw["wv_all"]]
        in_specs += [_full(aw["wk_all"]), _full(aw["pk"]), _full(aw["wv_all"])]
        out_specs += [_tok(hw, tb), _tok(hw, tb)]
        out_shape += [jax.ShapeDtypeStruct((bg, lg, hw), BF16)] * 2
    return pl.pallas_call(
        functools.partial(_kv_kernel, heads_out=heads_out),
        grid=(bg, lg // tb),
        in_specs=in_specs, out_specs=out_specs, out_shape=out_shape,
        compiler_params=_params("parallel", "parallel"),
        name="kv_proj",
    )(*ins)


def _q_kernel(x_ref, g_ref, sh_ref, sc_ref, wdq_ref, gcq_ref, wqa_ref, wqb_ref, cos_ref, sin_ref, q_ref,
              *, n_heads):
    h = _norm_mod(x_ref[0], g_ref[...], sh_ref[0], sc_ref[0])
    cq = _rms(_dot(h.astype(BF16), wdq_ref[...]), gcq_ref[...]).astype(BF16)
    qa = _dot(cq, wqa_ref[...])
    qb = _dot(cq, wqb_ref[...])
    cos = cos_ref[...]
    sin = sin_ref[...]
    for hd in range(n_heads):
        cols = slice(hd * LANES, (hd + 1) * LANES)
        q_ref[0, :, cols] = (qa[:, cols] * cos + qb[:, cols] * sin).astype(q_ref.dtype)


def _q_proj(x, g, sh, sc, lw, cos, sin, n_heads, out_dtype):
    bg, lg, d = x.shape
    tb = _tok_block(lg)
    hw = n_heads * LANES
    tab = _full(cos) if cos.shape[0] == 1 else pl.BlockSpec((tb, LANES), lambda b, i: (i, 0))
    return pl.pallas_call(
        functools.partial(_q_kernel, n_heads=n_heads),
        grid=(bg, lg // tb),
        in_specs=[_tok(d, tb), _full(g), _modspec(sh, tb), _modspec(sc, tb), _full(lw["w_dq"]),
                  _full(lw["g_cq"]), _full(lw["wq_a"]), _full(lw["wq_b"]), tab, tab],
        out_specs=_tok(hw, tb),
        out_shape=jax.ShapeDtypeStruct((bg, lg, hw), out_dtype),
        compiler_params=_params("parallel", "parallel"),
        name="q_proj",
    )(x, g, sh, sc, lw["w_dq"], lw["g_cq"], lw["wq_a"], lw["wq_b"], cos, sin)


def _flash_kernel(q_ref, k_ref, v_ref, o_ref, m_ref, l_ref, acc_ref, *, tq, hps):
    qi = pl.program_id(2)
    m_ref[...] = jnp.full(m_ref.shape, NEG_INF, F32)
    l_ref[...] = jnp.zeros_like(l_ref)
    acc_ref[...] = jnp.zeros_like(acc_ref)

    def update(j, masked):
        start = pl.multiple_of(j * tq, tq)
        for hd in range(hps):
            cols = slice(hd * LANES, (hd + 1) * LANES)
            k = k_ref[0, pl.ds(start, tq), cols]
            v = v_ref[0, pl.ds(start, tq), cols]
            s = _dot(q_ref[0, :, cols], k, _NT)
            if masked:
                row = lax.broadcasted_iota(jnp.int32, s.shape, 0)
                col = lax.broadcasted_iota(jnp.int32, s.shape, 1)
                s = jnp.where(col <= row, s, NEG_INF)
            m_old = m_ref[hd]
            m_new = jnp.maximum(m_old, jnp.max(s, axis=-1, keepdims=True))
            corr = jnp.exp2(m_old - m_new)
            p = jnp.exp2(s - pltpu.repeat(m_new, tq // LANES, axis=1))
            l_ref[hd] = l_ref[hd] * corr + jnp.sum(p, axis=-1, keepdims=True)
            acc_ref[hd] = acc_ref[hd] * corr + _dot(p.astype(BF16), v)
            m_ref[hd] = m_new

    def body(j, carry):
        update(j, False)
        return carry

    lax.fori_loop(0, qi, body, 0)
    update(qi, True)
    for hd in range(hps):
        o_ref[0, :, hd * LANES:(hd + 1) * LANES] = (acc_ref[hd] / l_ref[hd]).astype(o_ref.dtype)


def _flash_attention(q, k, v, n_heads):
    bsz, lg, _ = q.shape
    tq = _tok_block(lg)
    hps = 2 if n_heads % 2 == 0 else 1
    seq = pl.BlockSpec((1, lg, hps * LANES), lambda b, h, i: (b, 0, h))
    blk = pl.BlockSpec((1, tq, hps * LANES), lambda b, h, i: (b, i, h))
    stat = pltpu.VMEM((hps, tq, LANES), F32)
    return pl.pallas_call(
        functools.partial(_flash_kernel, tq=tq, hps=hps),
        grid=(bsz, n_heads // hps, lg // tq),
        in_specs=[blk, seq, seq],
        out_specs=blk,
        out_shape=jax.ShapeDtypeStruct(q.shape, BF16),
        scratch_shapes=[stat, stat, stat],
        compiler_params=_params("parallel", "parallel", "arbitrary"),
        name="flash_attention",
    )(q, k, v)


def _absorb_kernel(q_ref, wk_ref, qlat_ref, qpe_ref, *, n_heads, nope, rope):
    c_dim = wk_ref.shape[0]
    for hd in range(n_heads):
        qh = q_ref[:, hd * LANES:(hd + 1) * LANES]
        wk = wk_ref[:, hd * LANES:(hd + 1) * LANES]
        qlat_ref[:, hd * c_dim:(hd + 1) * c_dim] = _dot(qh.astype(BF16), wk, _NT)
        qpe_ref[:, hd * rope:(hd + 1) * rope] = qh[:, nope:nope + rope]


def _absorb(q, wk_all, n_heads, nope, rope):
    rows = q.shape[0]
    c_dim = wk_all.shape[0]
    return pl.pallas_call(
        functools.partial(_absorb_kernel, n_heads=n_heads, nope=nope, rope=rope),
        in_specs=[_full(q), _full(wk_all)],
        out_specs=[pl.BlockSpec((rows, n_heads * c_dim), lambda: (0, 0)),
                   pl.BlockSpec((rows, n_heads * rope), lambda: (0, 0))],
        out_shape=[jax.ShapeDtypeStruct((rows, n_heads * c_dim), F32),
                   jax.ShapeDtypeStruct((rows, n_heads * rope), F32)],
        compiler_params=pltpu.CompilerParams(vmem_limit_bytes=VMEM_LIMIT),
        name="q_absorb",
    )(q, wk_all)


def _decode_kernel(pt_ref, qlat_ref, qpe_ref, cnew_ref, rnew_ref, *rest, pages):
    kc_refs = rest[:pages]
    kr_refs = rest[pages:2 * pages]
    o_ref, m_ref, l_ref, acc_ref = rest[2 * pages:]
    c = pl.program_id(1)

    @pl.when(c == 0)
    def _():
        m_ref[...] = jnp.full(m_ref.shape, NEG_INF, F32)
        l_ref[...] = jnp.zeros_like(l_ref)
        acc_ref[...] = jnp.zeros_like(acc_ref)

    ql = qlat_ref[0]
    qp = qpe_ref[0]
    kc = jnp.concatenate([r[0].astype(BF16) for r in kc_refs], axis=0)
    kr = jnp.concatenate([r[0].astype(BF16) for r in kr_refs], axis=1)
    s = _dot(ql.astype(BF16), kc, _NT) + _dot(qp.astype(BF16), kr)
    m_old = m_ref[...]
    m_new = jnp.maximum(m_old, jnp.max(s, axis=-1, keepdims=True))
    corr = jnp.exp(m_old - m_new)
    p = jnp.exp(s - m_new)
    l_ref[...] = l_ref[...] * corr + jnp.sum(p, axis=-1, keepdims=True)
    acc_ref[...] = acc_ref[...] * corr + _dot(p.astype(BF16), kc)
    m_ref[...] = m_new

    @pl.when(c == pl.num_programs(1) - 1)
    def _():
        cn = cnew_ref[0]
        rn = rnew_ref[0]
        s_new = jnp.sum(ql * cn, axis=-1, keepdims=True) + jnp.sum(qp * rn, axis=-1, keepdims=True)
        m_old = m_ref[...]
        m_new = jnp.maximum(m_old, s_new)
        corr = jnp.exp(m_old - m_new)
        p_new = jnp.exp(s_new - m_new)
        l_new = l_ref[...] * corr + p_new
        o_ref[0] = (acc_ref[...] * corr + p_new * cn) / l_new


def _decode_attention(qlat, qpe, ckv_new, kr_new, cache_c, cache_rt, page_table):
    bd, n_heads, c_dim = qlat.shape
    r_dim = qpe.shape[2]
    page = cache_c.shape[1]
    n_pages = page_table.shape[1]
    pages = math.gcd(n_pages, 16)
    per_b3 = lambda w: pl.BlockSpec((1, n_heads, w), lambda b, c, pt: (b, 0, 0))
    new3 = lambda w: pl.BlockSpec((1, 1, w), lambda b, c, pt: (b, 0, 0))

    def page_spec(rows, w, j):
        return pl.BlockSpec((1, rows, w), lambda b, c, pt: (pt[b, c * pages + j], 0, 0))

    grid_spec = pltpu.PrefetchScalarGridSpec(
        num_scalar_prefetch=1,
        grid=(bd, n_pages // pages),
        in_specs=[per_b3(c_dim), per_b3(r_dim), new3(c_dim), new3(r_dim)]
        + [page_spec(page, c_dim, j) for j in range(pages)]
        + [page_spec(r_dim, page, j) for j in range(pages)],
        out_specs=per_b3(c_dim),
        scratch_shapes=[pltpu.VMEM((n_heads, 1), F32), pltpu.VMEM((n_heads, 1), F32),
                        pltpu.VMEM((n_heads, c_dim), F32)],
    )
    return pl.pallas_call(
        functools.partial(_decode_kernel, pages=pages),
        grid_spec=grid_spec,
        out_shape=jax.ShapeDtypeStruct((bd, n_heads, c_dim), F32),
        compiler_params=_params("parallel", "arbitrary"),
        name="decode_attention",
    )(page_table, qlat, qpe, ckv_new, kr_new, *([cache_c] * pages), *([cache_rt] * pages))


def _upproj_kernel(ol_ref, wv_ref, o_ref, *, n_heads):
    c_dim = wv_ref.shape[0]
    for hd in range(n_heads):
        ol = ol_ref[:, hd * c_dim:(hd + 1) * c_dim].astype(BF16)
        o_ref[0, :, hd * LANES:(hd + 1) * LANES] = _dot(ol, wv_ref[:, hd * LANES:(hd + 1) * LANES]).astype(BF16)


def _upproj(o_lat, wv_all, n_heads):
    rows = o_lat.shape[0]
    hw = wv_all.shape[1]
    return pl.pallas_call(
        functools.partial(_upproj_kernel, n_heads=n_heads),
        in_specs=[_full(o_lat), _full(wv_all)],
        out_specs=pl.BlockSpec((1, rows, hw), lambda: (0, 0, 0)),
        out_shape=jax.ShapeDtypeStruct((1, rows, hw), BF16),
        compiler_params=pltpu.CompilerParams(vmem_limit_bytes=VMEM_LIMIT),
        name="v_upproj",
    )(o_lat, wv_all)


def _out_res_kernel(o_ref, w_ref, x_ref, ga_ref, y_ref):
    y_ref[0] = x_ref[0] + ga_ref[0] * _dot(o_ref[0], w_ref[...])


def _out_residual(o, w_o_pad, x, ga):
    bg, lg, d = x.shape
    tb = _tok_block(lg)
    return pl.pallas_call(
        _out_res_kernel,
        grid=(bg, lg // tb),
        in_specs=[_tok(o.shape[2], tb), _full(w_o_pad), _tok(d, tb), _modspec(ga, tb)],
        out_specs=_tok(d, tb),
        out_shape=jax.ShapeDtypeStruct((bg, lg, d), F32),
        compiler_params=_params("parallel", "parallel"),
        name="out_residual",
    )(o, w_o_pad, x, ga)


def _final_kernel(x_ref, g_ref, sh_ref, sc_ref, y_ref):
    y_ref[0] = _norm_mod(x_ref[0], g_ref[...], sh_ref[0], sc_ref[0])


def _final_norm(x, g, sh, sc):
    bg, lg, d = x.shape
    tb = _tok_block(lg)
    return pl.pallas_call(
        _final_kernel,
        grid=(bg, lg // tb),
        in_specs=[_tok(d, tb), _full(g), _modspec(sh, tb), _modspec(sc, tb)],
        out_specs=_tok(d, tb),
        out_shape=jax.ShapeDtypeStruct((bg, lg, d), F32),
        compiler_params=_params("parallel", "parallel"),
        name="final_norm",
    )(x, g, sh, sc)


def _pad_heads(w, offset=0):
    pad = [(0, 0)] * (w.ndim - 1) + [(offset, LANES - offset - w.shape[-1])]
    wp = jnp.pad(w, pad)
    return wp.reshape(wp.shape[:-2] + (wp.shape[-2] * LANES,))


def _attn_weights(w_dkv, g_ckv, w_kr, w_ukv, nope, n_heads):
    r_dim = w_kr.shape[1]
    pk = jnp.tile(jnp.pad(jnp.eye(r_dim, dtype=F32), ((0, 0), (nope, LANES - nope - r_dim))), (1, n_heads))
    return dict(w_dkv=w_dkv, g_ckv=g_ckv.reshape(1, -1), w_kr=w_kr, w_kr_rot=_rot_cols(w_kr),
                wk_all=_pad_heads(w_ukv[..., :nope]).astype(BF16),
                wv_all=_pad_heads(w_ukv[..., nope:]).astype(BF16),
                pk=pk.astype(BF16))


def _query_weights(w_dq, g_cq, w_uq, w_o, nope, rope, n_heads):
    q_lora = w_uq.shape[0]
    wq = w_uq.reshape(q_lora, n_heads, nope + rope)
    w_pe = wq[..., nope:]
    wq_a = _pad_heads(wq)
    wq_b = _pad_heads(_rot_cols(w_pe), offset=nope)
    v_head = w_o.shape[0] // n_heads
    w_o_pad = jnp.pad(w_o.reshape(n_heads, v_head, -1), ((0, 0), (0, LANES - v_head), (0, 0)))
    return dict(w_dq=w_dq.astype(BF16), g_cq=g_cq.reshape(1, -1), wq_a=wq_a.astype(BF16),
                wq_b=wq_b.astype(BF16), w_o_pad=w_o_pad.reshape(n_heads * LANES, -1).astype(BF16))


def _query_tables(positions, nope, rope, scale):
    cos, sin = _rope_tables(positions, rope, scale)
    n = positions.shape[0]
    tail = LANES - nope - rope
    cos_t = jnp.concatenate([jnp.full((n, nope), scale, F32), cos, jnp.zeros((n, tail), F32)], -1)
    sin_t = jnp.concatenate([jnp.zeros((n, nope), F32), sin, jnp.zeros((n, tail), F32)], -1)
    return cos_t, sin_t


def kernel(x_prompt, x_sample, state_ssm_re, state_ssm_im, cache_kv_latent, cache_k_rope, page_table,
           c_prompt, c_sample,
           w_mod, b_mod, g_norm,
           ssm_a_re, ssm_a_im, ssm_log_dt, ssm_b_re, ssm_b_im, ssm_c_re, ssm_c_im, ssm_d, ssm_w_glu,
           w_mod_kv, b_mod_kv, g_kv_norm, w_dkv, g_ckv, w_kr, w_ukv,
           w_dq, g_cq, w_uq, w_o,
           peer_w_q, peer_keys, peer_u, peer_v,
           w_mod_final, b_mod_final, g_final):
    bsz, seq, d = x_prompt.shape
    bd, dec_seq, _ = x_sample.shape
    assert dec_seq == 1
    depth = w_mod.shape[0]
    n_a = ssm_a_re.shape[0]
    n_groups, n_state = ssm_a_re.shape[1], ssm_a_re.shape[2]
    n_heads = w_ukv.shape[1]
    rope = w_kr.shape[1]
    nope = w_uq.shape[2] // n_heads - rope
    page = cache_kv_latent.shape[1]
    past_len = page_table.shape[1] * page
    sm_scale = 1.0 / math.sqrt(nope + rope)

    c_all = jnp.concatenate([c_prompt, c_sample], axis=0)

    def grouped(m, n):
        parts = jnp.split(m, n, axis=-1)
        return [q[:bsz][:, None, :] for q in parts], [q[bsz:][None] for q in parts]

    mods = [grouped(_mod_linear(c_all, w_mod[l], b_mod[l]), 6) for l in range(depth)]
    mod_kv = grouped(_mod_linear(c_all, w_mod_kv, b_mod_kv), 2)
    mod_f = grouped(_mod_linear(c_all, w_mod_final, b_mod_final), 2)

    s5 = [_s5_prep(ssm_a_re[l], ssm_a_im[l], ssm_log_dt[l], ssm_b_re[l], ssm_b_im[l],
                   ssm_c_re[l], ssm_c_im[l]) for l in range(n_a)]
    glu_w = [ssm_w_glu[l].astype(BF16) for l in range(n_a)]
    peer_w = []
    for l in range(depth):
        wqh, wql = _hi_lo(peer_w_q[l])
        peer_w.append(dict(wqh=wqh, wql=wql, keys=peer_keys[l], u=peer_u[l].astype(BF16),
                           vt=peer_v[l].T.astype(BF16)))
    aw = _attn_weights(w_dkv, g_ckv, w_kr, w_ukv, nope, n_heads)
    qw = [_query_weights(w_dq[j], g_cq[j], w_uq[j], w_o[j], nope, rope, n_heads) for j in range(depth - n_a)]

    pos_p = jnp.arange(seq, dtype=jnp.int32)
    pos_s = past_len + jnp.arange(dec_seq, dtype=jnp.int32)
    tables = []
    for pos, q_scale in ((pos_p, sm_scale * math.log2(math.e)), (pos_s, sm_scale)):
        tables.append(dict(k=_rope_tables(pos, rope), q=_query_tables(pos, nope, rope, q_scale)))

    def gvec(v):
        return v.reshape(1, -1)

    def run(x, grp, h0_re, h0_im):
        is_prompt = grp == 0
        tab = tables[grp]
        ssm_re, ssm_im = [], []
        ckv = kr = kcat = vpad = None
        for layer in range(depth):
            sh1, sc1, ga1, sh2, sc2, ga2 = mods[layer][grp]
            if layer == n_a:
                mk = mod_kv[grp]
                outs = _kv_proj(x, gvec(g_kv_norm), mk[0], mk[1], aw, *tab["k"], heads_out=is_prompt)
                if is_prompt:
                    ckv, kr, kcat, vpad = outs
                else:
                    ckv, kr = outs
            if layer < n_a:
                if is_prompt:
                    z, hr, hi = _s5_prompt(x, gvec(g_norm[layer, 0]), sh1, sc1, s5[layer], gvec(ssm_d[layer]))
                else:
                    z, hr, hi = _s5_step(x, gvec(g_norm[layer, 0]), sh1, sc1, h0_re[layer], h0_im[layer],
                                         s5[layer], gvec(ssm_d[layer]))
                ssm_re.append(hr.reshape(-1, n_groups, n_state))
                ssm_im.append(hi.reshape(-1, n_groups, n_state))
                x = _glu_residual(z, glu_w[layer], x, ga1)
            else:
                lw = qw[layer - n_a]
                if is_prompt:
                    q = _q_proj(x, gvec(g_norm[layer, 0]), sh1, sc1, lw, *tab["q"], n_heads, BF16)
                    o = _flash_attention(q, kcat, vpad, n_heads)
                else:
                    q = _q_proj(x, gvec(g_norm[layer, 0]), sh1, sc1, lw, *tab["q"], n_heads, F32)
                    qlat, qpe = _absorb(q[0], aw["wk_all"], n_heads, nope, rope)
                    rows = qlat.shape[0]
                    o_lat = _decode_attention(qlat.reshape(rows, n_heads, -1), qpe.reshape(rows, n_heads, -1),
                                              ckv.reshape(rows, 1, -1), kr.reshape(rows, 1, -1),
                                              cache_kv_latent, cache_k_rope.transpose(0, 2, 1), page_table)
                    o = _upproj(o_lat.reshape(rows, -1), aw["wv_all"], n_heads)
                x = _out_residual(o, lw["w_o_pad"], x, ga1)
            x = _peer(x, gvec(g_norm[layer, 1]), sh2, sc2, ga2, peer_w[layer])
        shf, scf = mod_f[grp]
        y = _final_norm(x, gvec(g_final), shf, scf)
        return y, jnp.stack(ssm_re), jnp.stack(ssm_im), ckv, kr

    y_p, re_p, im_p, ckv_p, kr_p = run(x_prompt, 0, None, None)
    gn = n_groups * n_state
    h0_re = state_ssm_re.reshape(n_a, bd, gn)
    h0_im = state_ssm_im.reshape(n_a, bd, gn)
    y_s, re_s, im_s, ckv_s, kr_s = run(x_sample.reshape(1, bd, d), 1, h0_re, h0_im)
    return (y_p, y_s.reshape(bd, dec_seq, d), re_p, im_p, ckv_p, kr_p,
            re_s, im_s, ckv_s.reshape(bd, dec_seq, -1), kr_s.reshape(bd, dec_seq, -1))
```

```python
import functools
import math

import jax
import jax.numpy as jnp
from jax import lax
from jax.experimental import pallas as pl
from jax.experimental.pallas import tpu as pltpu

F32 = jnp.float32
BF16 = jnp.bfloat16
EPS = 1e-6
ROPE_THETA = 10000.0
PEER_TOPK = 16
NEG_INF = -1e30
LANES = 128
PEER_GATE_W = 512
PEER_ROUTE_W = 256
PEER_SUB_CHUNK = 1024
PEER_SUBS_PER_STEP = 2
DECODE_PAGES_PER_CHUNK = 64
DECODE_CHAINS = 1
SSM_GROUPS_PER_BLOCK = 16
VMEM_LIMIT = 56 * 1024 * 1024

_NN = (((1,), (0,)), ((), ()))
_NT = (((1,), (1,)), ((), ()))


def _dot(a, b, dims=_NN):
    return lax.dot_general(a, b, dims, preferred_element_type=F32)


def _split(a):
    hi = a.astype(BF16)
    lo = (a - hi.astype(F32)).astype(BF16)
    return hi, lo


def _dot3(a, b, dims=_NN):
    ah, al = _split(a)
    bh, bl = _split(b)
    return _dot(ah, bh, dims) + (_dot(ah, bl, dims) + _dot(al, bh, dims))


def _dot3w(a, wh, wl, dims=_NN):
    ah, al = _split(a)
    return _dot(ah, wh, dims) + (_dot(ah, wl, dims) + _dot(al, wh, dims))


def _sigmoid(x):
    return 1.0 / (1.0 + jnp.exp(-x))


def _rms(x, g):
    return x * lax.rsqrt(jnp.mean(x * x, axis=-1, keepdims=True) + EPS) * g


def _norm_mod(x, g, sh, sc):
    return _rms(x, g) * (1.0 + sc) + sh


def _params(*sem, flags=None):
    return pltpu.CompilerParams(dimension_semantics=sem, vmem_limit_bytes=VMEM_LIMIT, flags=flags)


def _full(a):
    nd = a.ndim
    return pl.BlockSpec(a.shape, lambda *_: (0,) * nd)


def _tok(width, tb):
    return pl.BlockSpec((1, tb, width), lambda b, i, *_: (b, i, 0))


def _modspec(m, tb):
    if m.shape[1] == 1:
        return pl.BlockSpec((1, 1, m.shape[2]), lambda b, i, *_: (b, 0, 0))
    return pl.BlockSpec((1, tb, m.shape[2]), lambda b, i, *_: (b, i, 0))


def _tok_block(lg, cap=512):
    tb = min(lg, cap)
    assert lg % tb == 0
    return tb


def _hi_lo(w):
    hi = lax.bitcast_convert_type(lax.bitcast_convert_type(w, jnp.uint32) & jnp.uint32(0xFFFF0000), F32)
    return hi.astype(BF16), (w - hi).astype(BF16)


def _mod_kernel(c_ref, wh_ref, wl_ref, b_ref, o_ref):
    c = c_ref[...]
    o_ref[...] = _dot3w(c * _sigmoid(c), wh_ref[...], wl_ref[...]) + b_ref[...]


def _mod_linear(c, w, b):
    bc, d = c.shape
    n = w.shape[1]
    tn = min(n, 1024)
    assert n % tn == 0
    wh, wl = _hi_lo(w)
    return pl.pallas_call(
        _mod_kernel,
        grid=(n // tn,),
        in_specs=[pl.BlockSpec((bc, d), lambda j: (0, 0)),
                  pl.BlockSpec((d, tn), lambda j: (0, j)),
                  pl.BlockSpec((d, tn), lambda j: (0, j)),
                  pl.BlockSpec((1, tn), lambda j: (0, j))],
        out_specs=pl.BlockSpec((bc, tn), lambda j: (0, j)),
        out_shape=jax.ShapeDtypeStruct((bc, n), F32),
        compiler_params=_params("parallel"),
        name="mod_linear",
    )(c, wh, wl, b.reshape(1, n))


def _s5_prep(a_re, a_im, log_dt, b_re, b_im, c_re, c_im):
    g, n, p = b_re.shape
    gb = min(g, SSM_GROUPS_PER_BLOCK)
    assert g % gb == 0
    k = g // gb
    dt = jnp.exp(log_dt.astype(F32))[:, None]
    a_re = a_re.astype(F32)
    a_im = a_im.astype(F32)
    er = jnp.exp(a_re * dt)
    lr = er * jnp.cos(a_im * dt)
    li = er * jnp.sin(a_im * dt)
    den = a_re * a_re + a_im * a_im
    kr = ((lr - 1.0) * a_re + li * a_im) / den
    ki = (li * a_re - (lr - 1.0) * a_im) / den
    bbr = kr[..., None] * b_re - ki[..., None] * b_im
    bbi = kr[..., None] * b_im + ki[..., None] * b_re
    eye = jnp.eye(gb, dtype=F32)

    def blk_in(b):
        bt = b.reshape(k, gb, n, p).transpose(0, 1, 3, 2)
        return (bt[:, :, :, None, :] * eye[None, :, None, :, None]).reshape(k, gb * p, gb * n)

    def blk_out(c):
        ct = c.reshape(k, gb, p, n).transpose(0, 1, 3, 2)
        return (ct[:, :, :, None, :] * eye[None, :, None, :, None]).reshape(k, gb * n, gb * p)

    b_blk = jnp.concatenate([blk_in(bbr), blk_in(bbi)], axis=-1)
    c_blk = jnp.concatenate([blk_out(c_re.astype(F32)), -blk_out(c_im.astype(F32))], axis=1)
    bh, bl = _hi_lo(b_blk)
    return dict(bh=bh, bl=bl, c=c_blk.astype(BF16), lam_re=lr.reshape(1, g * n), lam_im=li.reshape(1, g * n))


def _s5_input(h, bh_ref, bl_ref, k, w):
    hs = h[:, k * w:(k + 1) * w]
    return _dot3w(hs, bh_ref[k], bl_ref[k])


def _s5_prompt_kernel(x_ref, g_ref, sh_ref, sc_ref, bh_ref, bl_ref, c_ref, lre_ref, lim_ref, d_ref,
                      z_ref, sre_ref, sim_ref, s_re, s_im, st_re, st_im, h_ref, hh_ref, hl_ref,
                      *, nblk, tc, pitch, kper):
    j = pl.program_id(0)

    @pl.when(j == 0)
    def _():
        st_re[...] = jnp.zeros_like(st_re)
        st_im[...] = jnp.zeros_like(st_im)

    bsz, _, d = x_ref.shape
    w = d // nblk
    nw = lre_ref.shape[2]
    n_slabs = nw // LANES
    rows = kper * bsz
    for b in range(bsz):
        h = _norm_mod(x_ref[b], g_ref[...], sh_ref[b], sc_ref[b])
        h_ref[b] = h
        hh, hl = _split(h)
        hh_ref[b * tc:(b + 1) * tc, :] = hh
        hl_ref[b * tc:(b + 1) * tc, :] = hl

    for pair in range(nblk // kper):
        for kk in range(kper):
            k = pair * kper + kk
            hh = hh_ref[:, k * w:(k + 1) * w]
            hl = hl_ref[:, k * w:(k + 1) * w]
            bu = _dot(hh, bh_ref[k]) + (_dot(hh, bl_ref[k]) + _dot(hl, bh_ref[k]))
            for b in range(bsz):
                r0 = (kk * bsz + b) * pitch
                for sl in range(n_slabs):
                    s_re[sl, r0:r0 + tc, :] = bu[b * tc:(b + 1) * tc, sl * LANES:(sl + 1) * LANES]
                    s_im[sl, r0:r0 + tc, :] = bu[b * tc:(b + 1) * tc, nw + sl * LANES:nw + (sl + 1) * LANES]
        lr = lre_ref[pair]
        li = lim_ref[pair]

        def step(t, carry):
            sr, si = carry
            idx = pl.ds(t, rows, stride=pitch)
            br = jnp.concatenate([s_re[sl, idx, :] for sl in range(n_slabs)], axis=1)
            bi = jnp.concatenate([s_im[sl, idx, :] for sl in range(n_slabs)], axis=1)
            nr = lr * sr - li * si + br
            ni = lr * si + li * sr + bi
            for sl in range(n_slabs):
                s_re[sl, idx, :] = nr[:, sl * LANES:(sl + 1) * LANES]
                s_im[sl, idx, :] = ni[:, sl * LANES:(sl + 1) * LANES]
            return nr, ni

        sr, si = lax.fori_loop(0, tc, step, (st_re[pair], st_im[pair]))
        st_re[pair] = sr
        st_im[pair] = si
        for kk in range(kper):
            k = pair * kper + kk
            for b in range(bsz):
                r0 = (kk * bsz + b) * pitch
                hre = jnp.concatenate([s_re[sl, r0:r0 + tc, :] for sl in range(n_slabs)], axis=1)
                him = jnp.concatenate([s_im[sl, r0:r0 + tc, :] for sl in range(n_slabs)], axis=1)
                y = _dot(hre.astype(BF16), c_ref[k, :nw, :]) + _dot(him.astype(BF16), c_ref[k, nw:, :])
                hk = h_ref[b, :, k * w:(k + 1) * w]
                z_ref[b, :, k * w:(k + 1) * w] = jax.nn.gelu(y + d_ref[:, k * w:(k + 1) * w] * hk)
    sre_ref[...] = st_re[...]
    sim_ref[...] = st_im[...]


def _s5_prompt(x, g, sh, sc, prm, d_skip):
    bsz, lg, d = x.shape
    nblk = prm["bh"].shape[0]
    gn = prm["lam_re"].shape[1]
    nw = gn // nblk
    sublanes = 8
    assert sublanes % bsz == 0 and nblk % (sublanes // bsz) == 0 and nw % LANES == 0
    kper = sublanes // bsz
    npairs = nblk // kper
    tc = _tok_block(lg, 128)
    pitch = tc + 8 if (tc // 8) % 2 == 0 else tc
    rows = kper * bsz

    def pack_lam(v):
        return jnp.repeat(v.reshape(npairs, kper, nw), bsz, axis=1)

    lre, lim = pack_lam(prm["lam_re"]), pack_lam(prm["lam_im"])
    kern = functools.partial(_s5_prompt_kernel, nblk=nblk, tc=tc, pitch=pitch, kper=kper)
    blk = pl.BlockSpec((bsz, tc, d), lambda j: (0, j, 0))
    st_spec = pl.BlockSpec((npairs, rows, nw), lambda j: (0, 0, 0))
    st_shape = jax.ShapeDtypeStruct((npairs, rows, nw), F32)
    z, sre, sim = pl.pallas_call(
        kern,
        grid=(lg // tc,),
        in_specs=[blk, _full(g), _full(sh), _full(sc),
                  _full(prm["bh"]), _full(prm["bl"]), _full(prm["c"]), _full(lre), _full(lim), _full(d_skip)],
        out_specs=[blk, st_spec, st_spec],
        out_shape=[jax.ShapeDtypeStruct((bsz, lg, d), F32), st_shape, st_shape],
        scratch_shapes=[pltpu.VMEM((nw // LANES, rows * pitch, LANES), F32),
                        pltpu.VMEM((nw // LANES, rows * pitch, LANES), F32),
                        pltpu.VMEM((npairs, rows, nw), F32), pltpu.VMEM((npairs, rows, nw), F32),
                        pltpu.VMEM((bsz, tc, d), F32), pltpu.VMEM((bsz * tc, d), BF16),
                        pltpu.VMEM((bsz * tc, d), BF16)],
        compiler_params=_params("arbitrary"),
        name="s5_prompt",
    )(x, g, sh, sc, prm["bh"], prm["bl"], prm["c"], lre, lim, d_skip)

    def unpack(s):
        return s.reshape(npairs, kper, bsz, nw).transpose(2, 0, 1, 3).reshape(bsz, gn)

    return z, unpack(sre), unpack(sim)


def _s5_step_kernel(x_ref, g_ref, sh_ref, sc_ref, h0re_ref, h0im_ref, bh_ref, bl_ref, c_ref,
                    lre_ref, lim_ref, d_ref, z_ref, sre_ref, sim_ref, *, nblk):
    h = _norm_mod(x_ref[0], g_ref[...], sh_ref[0], sc_ref[0])
    w = h.shape[1] // nblk
    nw = lre_ref.shape[1] // nblk
    for k in range(nblk):
        cols = slice(k * nw, (k + 1) * nw)
        bu = _s5_input(h, bh_ref, bl_ref, k, w)
        lr = lre_ref[:, cols]
        li = lim_ref[:, cols]
        pr = h0re_ref[:, cols]
        pi = h0im_ref[:, cols]
        nr = lr * pr - li * pi + bu[:, :nw]
        ni = lr * pi + li * pr + bu[:, nw:]
        sre_ref[:, cols] = nr
        sim_ref[:, cols] = ni
        y = _dot(nr.astype(BF16), c_ref[k, :nw, :]) + _dot(ni.astype(BF16), c_ref[k, nw:, :])
        hk = h[:, k * w:(k + 1) * w]
        z_ref[0, :, k * w:(k + 1) * w] = jax.nn.gelu(y + d_ref[:, k * w:(k + 1) * w] * hk)


def _s5_step(x, g, sh, sc, h0_re, h0_im, prm, d_skip):
    _, rows, d = x.shape
    nblk = prm["bh"].shape[0]
    gn = prm["lam_re"].shape[1]
    tb = _tok_block(rows, 128)
    st_spec = pl.BlockSpec((tb, gn), lambda b, i: (i, 0))
    return pl.pallas_call(
        functools.partial(_s5_step_kernel, nblk=nblk),
        grid=(1, rows // tb),
        in_specs=[_tok(d, tb), _full(g), _modspec(sh, tb), _modspec(sc, tb), st_spec, st_spec,
                  _full(prm["bh"]), _full(prm["bl"]), _full(prm["c"]),
                  _full(prm["lam_re"]), _full(prm["lam_im"]), _full(d_skip)],
        out_specs=[_tok(d, tb), st_spec, st_spec],
        out_shape=[jax.ShapeDtypeStruct((1, rows, d), F32),
                   jax.ShapeDtypeStruct((rows, gn), F32),
                   jax.ShapeDtypeStruct((rows, gn), F32)],
        compiler_params=_params("parallel", "parallel"),
        name="s5_step",
    )(x, g, sh, sc, h0_re, h0_im, prm["bh"], prm["bl"], prm["c"], prm["lam_re"], prm["lam_im"], d_skip)


def _glu_res_kernel(z_ref, w_ref, x_ref, ga_ref, o_ref):
    y = _dot(z_ref[0].astype(BF16), w_ref[...])
    d = x_ref.shape[2]
    o_ref[0] = x_ref[0] + ga_ref[0] * (y[:, :d] * _sigmoid(y[:, d:]))


def _glu_residual(z, w_glu_bf, x, ga):
    bg, lg, d = x.shape
    tb = _tok_block(lg)
    return pl.pallas_call(
        _glu_res_kernel,
        grid=(bg, lg // tb),
        in_specs=[_tok(d, tb), _full(w_glu_bf), _tok(d, tb), _modspec(ga, tb)],
        out_specs=_tok(d, tb),
        out_shape=jax.ShapeDtypeStruct((bg, lg, d), F32),
        compiler_params=_params("parallel", "parallel"),
        name="glu_residual",
    )(z, w_glu_bf, x, ga)


def _cand_pairs(topk):
    return [(a, b) for a in range(topk + 1) for b in range(topk + 1) if (a + 1) * (b + 1) <= topk + 1]


def _top_rows(s, count, rank_count=0):
    out = []
    rank = jnp.zeros(s.shape, F32) if rank_count else None
    for r in range(count):
        if r == 0:
            cur = s
        else:
            below = s < out[-1]
            cur = jnp.where(below, s, NEG_INF)
            if r <= rank_count:
                rank = jnp.where(below, float(r), rank)
        out.append(jnp.max(cur, axis=0, keepdims=True))
    return out, rank


def _peer_route_kernel(x_ref, g_ref, sh_ref, sc_ref, wqh_ref, wql_ref, keys_ref,
                       hb_ref, r1_ref, e1_ref, nb_ref, coef_ref, cand_ref, *, topk):
    h = _norm_mod(x_ref[0], g_ref[...], sh_ref[0], sc_ref[0])
    hb_ref[0] = h.astype(BF16)
    q = _dot3w(h, wqh_ref[...], wql_ref[...])
    n_heads, _, n_keys, dk = keys_ref.shape
    pairs = _cand_pairs(topk)
    cand_ref[...] = jnp.full(cand_ref.shape, NEG_INF, F32)
    tb = q.shape[0]
    gw = r1_ref.shape[4]
    rw = cand_ref.shape[1]
    for hd, t0 in [(hd, t0) for hd in range(n_heads) for t0 in range(0, tb, rw)]:
        st, tops, rank1 = [], [], None
        for s in range(2):
            qs = q[t0:t0 + rw, (hd * 2 + s) * dk:(hd * 2 + s + 1) * dk]
            sc = _dot3(keys_ref[hd, s], qs, _NT)
            st.append(sc)
            top, rank = _top_rows(sc, topk + 1, rank_count=topk * s)
            tops.append(top)
            rank1 = rank
        for r, (a, b) in enumerate(pairs):
            cand_ref[pl.ds(r, 1), :] = tops[0][a] + tops[1][b]
        cand = cand_ref[...]
        best, _ = _top_rows(cand, topk + 1)
        tau = 0.5 * (best[topk - 1] + best[topk])
        mtot = tops[0][0] + tops[1][0]
        z = jnp.sum(jnp.where(cand >= tau, jnp.exp(cand - mtot), 0.0), axis=0, keepdims=True)
        nb = jnp.zeros(st[0].shape, F32)
        for b in range(topk):
            nb = jnp.where(st[0] >= tau - tops[1][b], float(b + 1), nb)
        outs = ((r1_ref, rank1.astype(BF16)), (e1_ref, jnp.exp(st[1] - tops[1][0]).astype(BF16)),
                (nb_ref, nb), (coef_ref, jnp.exp(st[0] - tops[0][0]) / z))
        for ref, val in outs:
            val = pltpu.bitcast(val, jnp.uint32) if val.dtype == BF16 else val
            if rw >= gw:
                for k in range(rw // gw):
                    ref[0, hd, t0 // gw + k] = val[:, k * gw:(k + 1) * gw]
            else:
                ref[0, hd, t0 // gw, :, t0 % gw:t0 % gw + rw] = val


def _peer_route(x, g, sh, sc, wqh, wql, keys, topk=PEER_TOPK):
    bg, lg, d = x.shape
    n_heads, _, n_keys, _ = keys.shape
    tb = _tok_block(lg)
    n_cand = -(-len(_cand_pairs(topk)) // 8) * 8
    gw = min(tb, PEER_GATE_W)

    def gate(rows):
        return pl.BlockSpec((1, n_heads, tb // gw, rows, gw), lambda b, i: (b, 0, i, 0, 0))

    def gate_shape(rows, dt):
        return jax.ShapeDtypeStruct((bg, n_heads, lg // gw, rows, gw), dt)

    packed = n_keys // 2
    gate_shapes = [gate_shape(packed, jnp.uint32), gate_shape(packed, jnp.uint32),
                   gate_shape(n_keys, F32), gate_shape(n_keys, F32)]
    return pl.pallas_call(
        functools.partial(_peer_route_kernel, topk=topk),
        grid=(bg, lg // tb),
        in_specs=[_tok(d, tb), _full(g), _modspec(sh, tb), _modspec(sc, tb),
                  _full(wqh), _full(wql), _full(keys)],
        out_specs=[_tok(d, tb), gate(packed), gate(packed), gate(n_keys), gate(n_keys)],
        out_shape=[jax.ShapeDtypeStruct((bg, lg, d), BF16)] + gate_shapes,
        scratch_shapes=[pltpu.VMEM((n_cand, min(tb, PEER_ROUTE_W)), F32)],
        compiler_params=_params("parallel", "parallel"),
        name="peer_route",
    )(x, g, sh, sc, wqh, wql, keys)


def _peer_dense_kernel(hb_ref, r1_ref, e1_ref, nb_ref, coef_ref, u_ref, vt_ref, x_ref, ga_ref,
                       o_ref, acc_ref, *gate_refs, n_keys):
    c = pl.program_id(2)
    n_steps = pl.num_programs(2)
    n_heads = r1_ref.shape[1]
    gw = r1_ref.shape[4]
    tb = hb_ref.shape[1]
    n_sub = len(gate_refs)
    ec = u_ref.shape[0] // n_sub
    per = ec // n_keys

    def build_gate(i0, n_i, dst):
        for ii in range(n_i):
            for lt in range(tb // gw):
                gate = None
                for hd in range(n_heads):
                    nb = nb_ref[0, hd, lt, pl.ds(i0 + ii, 1), :].astype(BF16)
                    coef = coef_ref[0, hd, lt, pl.ds(i0 + ii, 1), :].astype(BF16)
                    r1 = pltpu.bitcast(r1_ref[0, hd, lt], BF16)
                    e1 = pltpu.bitcast(e1_ref[0, hd, lt], BF16)
                    term = jnp.where(r1 < nb, e1 * coef, jnp.zeros((), BF16))
                    gate = term if gate is None else gate + term
                dst[ii * n_keys:(ii + 1) * n_keys, lt * gw:(lt + 1) * gw] = gate

    @pl.when(c == 0)
    def _():
        acc_ref[...] = jnp.zeros_like(acc_ref)

    for s in range(n_sub):
        rows = slice(s * ec, (s + 1) * ec)
        build_gate((c * n_sub + s) * per, per, gate_refs[s])
        act = jax.nn.gelu(_dot(u_ref[rows, :], hb_ref[0], _NT).astype(BF16))
        acc_ref[...] += _dot(vt_ref[:, rows], act * gate_refs[s][...])

    @pl.when(c == n_steps - 1)
    def _():
        o_ref[0] = x_ref[0] + ga_ref[0] * acc_ref[...].T


def _peer_dense(hb, gates, u_bf, vt_bf, x, ga):
    bg, lg, d = x.shape
    n_heads, n_keys = gates[2].shape[1], gates[2].shape[3]
    n_exp = u_bf.shape[0]
    tb = _tok_block(lg)
    sub = min(n_exp, PEER_SUB_CHUNK)
    n_sub = math.gcd(n_exp // sub, PEER_SUBS_PER_STEP)
    ec = n_sub * sub
    assert n_exp % ec == 0 and sub % n_keys == 0

    def gate(a):
        gw = a.shape[4]
        return pl.BlockSpec((1, n_heads, tb // gw, a.shape[3], gw), lambda b, i, c: (b, 0, i, 0, 0))

    return pl.pallas_call(
        functools.partial(_peer_dense_kernel, n_keys=n_keys),
        grid=(bg, lg // tb, n_exp // ec),
        in_specs=[_tok(d, tb)] + [gate(a) for a in gates] + [
                  pl.BlockSpec((ec, d), lambda b, i, c: (c, 0)),
                  pl.BlockSpec((d, ec), lambda b, i, c: (0, c)),
                  _tok(d, tb), _modspec(ga, tb)],
        out_specs=_tok(d, tb),
        out_shape=jax.ShapeDtypeStruct((bg, lg, d), F32),
        scratch_shapes=[pltpu.VMEM((d, tb), F32)] + [pltpu.VMEM((sub, tb), BF16)] * n_sub,
        compiler_params=_params("parallel", "parallel", "arbitrary"),
        name="peer_dense",
    )(hb, *gates, u_bf, vt_bf, x, ga)


def _peer(x, g, sh, sc, ga, pw):
    hb, *gates = _peer_route(x, g, sh, sc, pw["wqh"], pw["wql"], pw["keys"])
    return _peer_dense(hb, gates, pw["u"], pw["vt"], x, ga)


def _rope_tables(positions, rope_dim, scale=1.0):
    inv = ROPE_THETA ** (-jnp.arange(0, rope_dim, 2, dtype=F32) / rope_dim)
    ang = positions.astype(F32)[:, None] * inv
    cos, sin = jnp.cos(ang), jnp.sin(ang)
    return jnp.concatenate([cos, cos], -1) * scale, jnp.concatenate([sin, sin], -1) * scale


def _rot_cols(w):
    half = w.shape[-1] // 2
    return jnp.concatenate([-w[..., half:], w[..., :half]], axis=-1)


def _kv_kernel(x_ref, g_ref, sh_ref, sc_ref, wdkv_ref, gckv_ref, wkr_ref, wkrr_ref, cos_ref, sin_ref,
               *rest, heads_out):
    h = _norm_mod(x_ref[0], g_ref[...], sh_ref[0], sc_ref[0])
    ckv = _rms(_dot3(h, wdkv_ref[...]), gckv_ref[...])
    kr = _dot3(h, wkr_ref[...]) * cos_ref[...] + _dot3(h, wkrr_ref[...]) * sin_ref[...]
    if heads_out:
        wk_ref, pk_ref, wv_ref, ckv_ref, kr_ref, kcat_ref, v_ref = rest
        cb = ckv.astype(BF16)
        kcat_ref[0] = (_dot(cb, wk_ref[...]) + _dot(kr.astype(BF16), pk_ref[...])).astype(BF16)
        v_ref[0] = _dot(cb, wv_ref[...]).astype(BF16)
    else:
        ckv_ref, kr_ref = rest
    ckv_ref[0] = ckv
    kr_ref[0] = kr


def _kv_proj(x, g, sh, sc, aw, cos, sin, heads_out):
    bg, lg, d = x.shape
    tb = _tok_block(lg)
    c_dim, r_dim = aw["w_dkv"].shape[1], aw["w_kr"].shape[1]
    tab = _full(cos) if cos.shape[0] == 1 else pl.BlockSpec((tb, r_dim), lambda b, i: (i, 0))
    ins = [x, g, sh, sc, aw["w_dkv"], aw["g_ckv"], aw["w_kr"], aw["w_kr_rot"], cos, sin]
    in_specs = [_tok(d, tb), _full(g), _modspec(sh, tb), _modspec(sc, tb), _full(aw["w_dkv"]),
                _full(aw["g_ckv"]), _full(aw["w_kr"]), _full(aw["w_kr_rot"]), tab, tab]
    out_specs = [_tok(c_dim, tb), _tok(r_dim, tb)]
    out_shape = [jax.ShapeDtypeStruct((bg, lg, c_dim), F32), jax.ShapeDtypeStruct((bg, lg, r_dim), F32)]
    if heads_out:
        hw = aw["wk_all"].shape[1]
        ins += [aw["wk_all"], aw["pk"], aw["wv_all"]]
        in_specs += [_full(aw["wk_all"]), _full(aw["pk"]), _full(aw["wv_all"])]
        out_specs += [_tok(hw, tb), _tok(hw, tb)]
        out_shape += [jax.ShapeDtypeStruct((bg, lg, hw), BF16)] * 2
    return pl.pallas_call(
        functools.partial(_kv_kernel, heads_out=heads_out),
        grid=(bg, lg // tb),
        in_specs=in_specs, out_specs=out_specs, out_shape=out_shape,
        compiler_params=_params("parallel", "parallel"),
        name="kv_proj",
    )(*ins)


def _q_kernel(x_ref, g_ref, sh_ref, sc_ref, wdq_ref, gcq_ref, wqa_ref, wqb_ref, cos_ref, sin_ref, q_ref,
              *, n_heads):
    h = _norm_mod(x_ref[0], g_ref[...], sh_ref[0], sc_ref[0])
    cq = _rms(_dot(h.astype(BF16), wdq_ref[...]), gcq_ref[...]).astype(BF16)
    qa = _dot(cq, wqa_ref[...])
    qb = _dot(cq, wqb_ref[...])
    cos = cos_ref[...]
    sin = sin_ref[...]
    for hd in range(n_heads):
        cols = slice(hd * LANES, (hd + 1) * LANES)
        q_ref[0, :, cols] = (qa[:, cols] * cos + qb[:, cols] * sin).astype(q_ref.dtype)


def _q_proj(x, g, sh, sc, lw, cos, sin, n_heads, out_dtype):
    bg, lg, d = x.shape
    tb = _tok_block(lg)
    hw = n_heads * LANES
    tab = _full(cos) if cos.shape[0] == 1 else pl.BlockSpec((tb, LANES), lambda b, i: (i, 0))
    return pl.pallas_call(
        functools.partial(_q_kernel, n_heads=n_heads),
        grid=(bg, lg // tb),
        in_specs=[_tok(d, tb), _full(g), _modspec(sh, tb), _modspec(sc, tb), _full(lw["w_dq"]),
                  _full(lw["g_cq"]), _full(lw["wq_a"]), _full(lw["wq_b"]), tab, tab],
        out_specs=_tok(hw, tb),
        out_shape=jax.ShapeDtypeStruct((bg, lg, hw), out_dtype),
        compiler_params=_params("parallel", "parallel"),
        name="q_proj",
    )(x, g, sh, sc, lw["w_dq"], lw["g_cq"], lw["wq_a"], lw["wq_b"], cos, sin)


def _flash_kernel(q_ref, k_ref, v_ref, o_ref, m_ref, l_ref, acc_ref, *, tq, hps):
    qi = pl.program_id(2)
    m_ref[...] = jnp.full(m_ref.shape, NEG_INF, F32)
    l_ref[...] = jnp.zeros_like(l_ref)
    acc_ref[...] = jnp.zeros_like(acc_ref)

    def update(j, masked):
        start = pl.multiple_of(j * tq, tq)
        for hd in range(hps):
            cols = slice(hd * LANES, (hd + 1) * LANES)
            k = k_ref[0, pl.ds(start, tq), cols]
            v = v_ref[0, pl.ds(start, tq), cols]
            s = _dot(q_ref[0, :, cols], k, _NT)
            if masked:
                row = lax.broadcasted_iota(jnp.int32, s.shape, 0)
                col = lax.broadcasted_iota(jnp.int32, s.shape, 1)
                s = jnp.where(col <= row, s, NEG_INF)
            m_old = m_ref[hd]
            m_new = jnp.maximum(m_old, jnp.max(s, axis=-1, keepdims=True))
            corr = jnp.exp2(m_old - m_new)
            p = jnp.exp2(s - jnp.concatenate([m_new] * (tq // LANES), axis=1))
            l_ref[hd] = l_ref[hd] * corr + jnp.sum(p, axis=-1, keepdims=True)
            acc_ref[hd] = acc_ref[hd] * corr + _dot(p.astype(BF16), v)
            m_ref[hd] = m_new

    def body(j, carry):
        update(j, False)
        return carry

    lax.fori_loop(0, qi, body, 0)
    update(qi, True)
    for hd in range(hps):
        o_ref[0, :, hd * LANES:(hd + 1) * LANES] = (acc_ref[hd] / l_ref[hd]).astype(o_ref.dtype)


def _flash_attention(q, k, v, n_heads):
    bsz, lg, _ = q.shape
    tq = _tok_block(lg)
    hps = 2 if n_heads % 2 == 0 else 1
    seq = pl.BlockSpec((1, lg, hps * LANES), lambda b, h, i: (b, 0, h))
    blk = pl.BlockSpec((1, tq, hps * LANES), lambda b, h, i: (b, i, h))
    stat = pltpu.VMEM((hps, tq, LANES), F32)
    return pl.pallas_call(
        functools.partial(_flash_kernel, tq=tq, hps=hps),
        grid=(bsz, n_heads // hps, lg // tq),
        in_specs=[blk, seq, seq],
        out_specs=blk,
        out_shape=jax.ShapeDtypeStruct(q.shape, BF16),
        scratch_shapes=[stat, stat, stat],
        compiler_params=_params("parallel", "parallel", "arbitrary"),
        name="flash_attention",
    )(q, k, v)


def _absorb_kernel(q_ref, wk_ref, qlat_ref, qpe_ref, *, n_heads, nope, rope):
    c_dim = wk_ref.shape[0]
    for hd in range(n_heads):
        qh = q_ref[:, hd * LANES:(hd + 1) * LANES]
        wk = wk_ref[:, hd * LANES:(hd + 1) * LANES]
        qlat_ref[:, hd * c_dim:(hd + 1) * c_dim] = _dot(qh.astype(BF16), wk, _NT)
        qpe_ref[:, hd * rope:(hd + 1) * rope] = qh[:, nope:nope + rope]


def _absorb(q, wk_all, n_heads, nope, rope):
    rows = q.shape[0]
    c_dim = wk_all.shape[0]
    return pl.pallas_call(
        functools.partial(_absorb_kernel, n_heads=n_heads, nope=nope, rope=rope),
        in_specs=[_full(q), _full(wk_all)],
        out_specs=[pl.BlockSpec((rows, n_heads * c_dim), lambda: (0, 0)),
                   pl.BlockSpec((rows, n_heads * rope), lambda: (0, 0))],
        out_shape=[jax.ShapeDtypeStruct((rows, n_heads * c_dim), F32),
                   jax.ShapeDtypeStruct((rows, n_heads * rope), F32)],
        compiler_params=pltpu.CompilerParams(vmem_limit_bytes=VMEM_LIMIT),
        name="q_absorb",
    )(q, wk_all)


def _decode_kernel(pt_ref, qlat_ref, qpe_ref, cnew_ref, rnew_ref, cc_hbm, cr_hbm, o_ref,
                   kc_buf, kr_buf, sem_c, sem_r, *, pages, n_chunks, chains):
    b = pl.program_id(0)
    n_rows = pl.num_programs(0)

    def chunk_copies(row, chunk, slot, src_page=None):
        cps = []
        for j in range(pages):
            pg = pt_ref[row, chunk * pages + j] if src_page is None else src_page
            cps.append(pltpu.make_async_copy(cc_hbm.at[pg], kc_buf.at[slot, j], sem_c.at[slot]))
            cps.append(pltpu.make_async_copy(cr_hbm.at[pg], kr_buf.at[slot, j], sem_r.at[slot]))
        return cps

    def start_chunk(row, chunk, slot):
        for cp in chunk_copies(row, chunk, slot):
            cp.start()

    def wait_chunk(slot):
        for cp in chunk_copies(0, 0, slot, src_page=0):
            cp.wait()

    @pl.when(b == 0)
    def _():
        start_chunk(0, 0, 0)

    ql = qlat_ref[0]
    qp = qpe_ref[0]
    qlb = ql.astype(BF16)
    qpb = qp.astype(BF16)
    per = pages // chains
    n_heads, c_dim = ql.shape
    state = [(jnp.full((n_heads, 1), NEG_INF, F32), jnp.zeros((n_heads, 1), F32),
              jnp.zeros((n_heads, c_dim), F32)) for _ in range(chains)]
    for k in range(n_chunks):
        slot = k % 2
        if k + 1 < n_chunks:
            start_chunk(b, k + 1, 1 - slot)
        else:
            @pl.when(b + 1 < n_rows)
            def _():
                start_chunk(b + 1, 0, 1 - slot)
        wait_chunk(slot)
        for ch in range(chains):
            js = range(ch * per, (ch + 1) * per)
            kc = jnp.concatenate([kc_buf[slot, j].astype(BF16) for j in js], axis=0)
            kr = jnp.concatenate([kr_buf[slot, j].astype(BF16) for j in js], axis=1)
            s = _dot(qlb, kc, _NT) + _dot(qpb, kr)
            m_old, l_old, acc = state[ch]
            m_new = jnp.maximum(m_old, jnp.max(s, axis=-1, keepdims=True))
            corr = jnp.exp(m_old - m_new)
            p = jnp.exp(s - m_new)
            state[ch] = (m_new, l_old * corr + jnp.sum(p, axis=-1, keepdims=True),
                         acc * corr + _dot(p.astype(BF16), kc))

    cn = cnew_ref[0]
    rn = rnew_ref[0]
    s_new = jnp.sum(ql * cn, axis=-1, keepdims=True) + jnp.sum(qp * rn, axis=-1, keepdims=True)
    m_fin = s_new
    for m, _, _ in state:
        m_fin = jnp.maximum(m_fin, m)
    p_new = jnp.exp(s_new - m_fin)
    l_fin = p_new
    acc_fin = p_new * cn
    for m, l, acc in state:
        w = jnp.exp(m - m_fin)
        l_fin = l_fin + l * w
        acc_fin = acc_fin + acc * w
    o_ref[0] = acc_fin / l_fin


def _decode_attention(qlat, qpe, ckv_new, kr_new, cache_c, cache_rt, page_table):
    bd, n_heads, c_dim = qlat.shape
    r_dim = qpe.shape[2]
    page = cache_c.shape[1]
    n_pages = page_table.shape[1]
    pages = max(p for p in range(1, DECODE_PAGES_PER_CHUNK + 1)
                if n_pages % p == 0 and (n_pages // p) % 2 == 0)
    chains = math.gcd(pages, DECODE_CHAINS)
    per_b3 = lambda w: pl.BlockSpec((1, n_heads, w), lambda b, pt: (b, 0, 0))
    new3 = lambda w: pl.BlockSpec((1, 1, w), lambda b, pt: (b, 0, 0))
    grid_spec = pltpu.PrefetchScalarGridSpec(
        num_scalar_prefetch=1,
        grid=(bd,),
        in_specs=[per_b3(c_dim), per_b3(r_dim), new3(c_dim), new3(r_dim),
                  pl.BlockSpec(memory_space=pl.ANY), pl.BlockSpec(memory_space=pl.ANY)],
        out_specs=per_b3(c_dim),
        scratch_shapes=[pltpu.VMEM((2, pages, page, c_dim), F32), pltpu.VMEM((2, pages, r_dim, page), F32),
                        pltpu.SemaphoreType.DMA((2,)), pltpu.SemaphoreType.DMA((2,))],
    )
    return pl.pallas_call(
        functools.partial(_decode_kernel, pages=pages, n_chunks=n_pages // pages, chains=chains),
        grid_spec=grid_spec,
        out_shape=jax.ShapeDtypeStruct((bd, n_heads, c_dim), F32),
        compiler_params=_params("arbitrary"),
        name="decode_attention",
    )(page_table, qlat, qpe, ckv_new, kr_new, cache_c, cache_rt)


def _upproj_kernel(ol_ref, wv_ref, o_ref, *, n_heads):
    c_dim = wv_ref.shape[0]
    for hd in range(n_heads):
        ol = ol_ref[:, hd * c_dim:(hd + 1) * c_dim].astype(BF16)
        o_ref[0, :, hd * LANES:(hd + 1) * LANES] = _dot(ol, wv_ref[:, hd * LANES:(hd + 1) * LANES]).astype(BF16)


def _upproj(o_lat, wv_all, n_heads):
    rows = o_lat.shape[0]
    hw = wv_all.shape[1]
    return pl.pallas_call(
        functools.partial(_upproj_kernel, n_heads=n_heads),
        in_specs=[_full(o_lat), _full(wv_all)],
        out_specs=pl.BlockSpec((1, rows, hw), lambda: (0, 0, 0)),
        out_shape=jax.ShapeDtypeStruct((1, rows, hw), BF16),
        compiler_params=pltpu.CompilerParams(vmem_limit_bytes=VMEM_LIMIT),
        name="v_upproj",
    )(o_lat, wv_all)


def _out_res_kernel(o_ref, w_ref, x_ref, ga_ref, y_ref):
    y_ref[0] = x_ref[0] + ga_ref[0] * _dot(o_ref[0], w_ref[...])


def _out_residual(o, w_o_pad, x, ga):
    bg, lg, d = x.shape
    tb = _tok_block(lg)
    return pl.pallas_call(
        _out_res_kernel,
        grid=(bg, lg // tb),
        in_specs=[_tok(o.shape[2], tb), _full(w_o_pad), _tok(d, tb), _modspec(ga, tb)],
        out_specs=_tok(d, tb),
        out_shape=jax.ShapeDtypeStruct((bg, lg, d), F32),
        compiler_params=_params("parallel", "parallel"),
        name="out_residual",
    )(o, w_o_pad, x, ga)


def _final_kernel(x_ref, g_ref, sh_ref, sc_ref, y_ref):
    y_ref[0] = _norm_mod(x_ref[0], g_ref[...], sh_ref[0], sc_ref[0])


def _final_norm(x, g, sh, sc):
    bg, lg, d = x.shape
    tb = _tok_block(lg)
    return pl.pallas_call(
        _final_kernel,
        grid=(bg, lg // tb),
        in_specs=[_tok(d, tb), _full(g), _modspec(sh, tb), _modspec(sc, tb)],
        out_specs=_tok(d, tb),
        out_shape=jax.ShapeDtypeStruct((bg, lg, d), F32),
        compiler_params=_params("parallel", "parallel"),
        name="final_norm",
    )(x, g, sh, sc)


def _pad_heads(w, offset=0):
    pad = [(0, 0)] * (w.ndim - 1) + [(offset, LANES - offset - w.shape[-1])]
    wp = jnp.pad(w, pad)
    return wp.reshape(wp.shape[:-2] + (wp.shape[-2] * LANES,))


def _attn_weights(w_dkv, g_ckv, w_kr, w_ukv, nope, n_heads):
    r_dim = w_kr.shape[1]
    pk = jnp.tile(jnp.pad(jnp.eye(r_dim, dtype=F32), ((0, 0), (nope, LANES - nope - r_dim))), (1, n_heads))
    return dict(w_dkv=w_dkv, g_ckv=g_ckv.reshape(1, -1), w_kr=w_kr, w_kr_rot=_rot_cols(w_kr),
                wk_all=_pad_heads(w_ukv[..., :nope]).astype(BF16),
                wv_all=_pad_heads(w_ukv[..., nope:]).astype(BF16),
                pk=pk.astype(BF16))


def _query_weights(w_dq, g_cq, w_uq, w_o, nope, rope, n_heads):
    q_lora = w_uq.shape[0]
    wq = w_uq.reshape(q_lora, n_heads, nope + rope)
    w_pe = wq[..., nope:]
    wq_a = _pad_heads(wq)
    wq_b = _pad_heads(_rot_cols(w_pe), offset=nope)
    v_head = w_o.shape[0] // n_heads
    w_o_pad = jnp.pad(w_o.reshape(n_heads, v_head, -1), ((0, 0), (0, LANES - v_head), (0, 0)))
    return dict(w_dq=w_dq.astype(BF16), g_cq=g_cq.reshape(1, -1), wq_a=wq_a.astype(BF16),
                wq_b=wq_b.astype(BF16), w_o_pad=w_o_pad.reshape(n_heads * LANES, -1).astype(BF16))


def _query_tables(positions, nope, rope, scale):
    cos, sin = _rope_tables(positions, rope, scale)
    n = positions.shape[0]
    tail = LANES - nope - rope
    cos_t = jnp.concatenate([jnp.full((n, nope), scale, F32), cos, jnp.zeros((n, tail), F32)], -1)
    sin_t = jnp.concatenate([jnp.zeros((n, nope), F32), sin, jnp.zeros((n, tail), F32)], -1)
    return cos_t, sin_t


def kernel(x_prompt, x_sample, state_ssm_re, state_ssm_im, cache_kv_latent, cache_k_rope, page_table,
           c_prompt, c_sample,
           w_mod, b_mod, g_norm,
           ssm_a_re, ssm_a_im, ssm_log_dt, ssm_b_re, ssm_b_im, ssm_c_re, ssm_c_im, ssm_d, ssm_w_glu,
           w_mod_kv, b_mod_kv, g_kv_norm, w_dkv, g_ckv, w_kr, w_ukv,
           w_dq, g_cq, w_uq, w_o,
           peer_w_q, peer_keys, peer_u, peer_v,
           w_mod_final, b_mod_final, g_final):
    bsz, seq, d = x_prompt.shape
    bd, dec_seq, _ = x_sample.shape
    assert dec_seq == 1
    depth = w_mod.shape[0]
    n_a = ssm_a_re.shape[0]
    n_groups, n_state = ssm_a_re.shape[1], ssm_a_re.shape[2]
    n_heads = w_ukv.shape[1]
    rope = w_kr.shape[1]
    nope = w_uq.shape[2] // n_heads - rope
    page = cache_kv_latent.shape[1]
    past_len = page_table.shape[1] * page
    sm_scale = 1.0 / math.sqrt(nope + rope)

    c_all = jnp.concatenate([c_prompt, c_sample], axis=0)

    def grouped(m, n):
        parts = jnp.split(m, n, axis=-1)
        return [q[:bsz][:, None, :] for q in parts], [q[bsz:][None] for q in parts]

    mods = [grouped(_mod_linear(c_all, w_mod[l], b_mod[l]), 6) for l in range(depth)]
    mod_kv = grouped(_mod_linear(c_all, w_mod_kv, b_mod_kv), 2)
    mod_f = grouped(_mod_linear(c_all, w_mod_final, b_mod_final), 2)

    s5 = [_s5_prep(ssm_a_re[l], ssm_a_im[l], ssm_log_dt[l], ssm_b_re[l], ssm_b_im[l],
                   ssm_c_re[l], ssm_c_im[l]) for l in range(n_a)]
    glu_w = [ssm_w_glu[l].astype(BF16) for l in range(n_a)]
    peer_w = []
    for l in range(depth):
        wqh, wql = _hi_lo(peer_w_q[l])
        peer_w.append(dict(wqh=wqh, wql=wql, keys=peer_keys[l], u=peer_u[l].astype(BF16),
                           vt=peer_v[l].T.astype(BF16)))
    aw = _attn_weights(w_dkv, g_ckv, w_kr, w_ukv, nope, n_heads)
    qw = [_query_weights(w_dq[j], g_cq[j], w_uq[j], w_o[j], nope, rope, n_heads) for j in range(depth - n_a)]

    pos_p = jnp.arange(seq, dtype=jnp.int32)
    pos_s = past_len + jnp.arange(dec_seq, dtype=jnp.int32)
    tables = []
    for pos, q_scale in ((pos_p, sm_scale * math.log2(math.e)), (pos_s, sm_scale)):
        tables.append(dict(k=_rope_tables(pos, rope), q=_query_tables(pos, nope, rope, q_scale)))

    def gvec(v):
        return v.reshape(1, -1)

    def run(x, grp, h0_re, h0_im):
        is_prompt = grp == 0
        tab = tables[grp]
        ssm_re, ssm_im = [], []
        ckv = kr = kcat = vpad = None
        for layer in range(depth):
            sh1, sc1, ga1, sh2, sc2, ga2 = mods[layer][grp]
            if layer == n_a:
                mk = mod_kv[grp]
                outs = _kv_proj(x, gvec(g_kv_norm), mk[0], mk[1], aw, *tab["k"], heads_out=is_prompt)
                if is_prompt:
                    ckv, kr, kcat, vpad = outs
                else:
                    ckv, kr = outs
            if layer < n_a:
                if is_prompt:
                    z, hr, hi = _s5_prompt(x, gvec(g_norm[layer, 0]), sh1, sc1, s5[layer], gvec(ssm_d[layer]))
                else:
                    z, hr, hi = _s5_step(x, gvec(g_norm[layer, 0]), sh1, sc1, h0_re[layer], h0_im[layer],
                                         s5[layer], gvec(ssm_d[layer]))
                ssm_re.append(hr.reshape(-1, n_groups, n_state))
                ssm_im.append(hi.reshape(-1, n_groups, n_state))
                x = _glu_residual(z, glu_w[layer], x, ga1)
            else:
                lw = qw[layer - n_a]
                if is_prompt:
                    q = _q_proj(x, gvec(g_norm[layer, 0]), sh1, sc1, lw, *tab["q"], n_heads, BF16)
                    o = _flash_attention(q, kcat, vpad, n_heads)
                else:
                    q = _q_proj(x, gvec(g_norm[layer, 0]), sh1, sc1, lw, *tab["q"], n_heads, F32)
                    qlat, qpe = _absorb(q[0], aw["wk_all"], n_heads, nope, rope)
                    rows = qlat.shape[0]
                    o_lat = _decode_attention(qlat.reshape(rows, n_heads, -1), qpe.reshape(rows, n_heads, -1),
                                              ckv.reshape(rows, 1, -1), kr.reshape(rows, 1, -1),
                                              cache_kv_latent, cache_k_rope.transpose(0, 2, 1), page_table)
                    o = _upproj(o_lat.reshape(rows, -1), aw["wv_all"], n_heads)
                x = _out_residual(o, lw["w_o_pad"], x, ga1)
            x = _peer(x, gvec(g_norm[layer, 1]), sh2, sc2, ga2, peer_w[layer])
        shf, scf = mod_f[grp]
        y = _final_norm(x, gvec(g_final), shf, scf)
        return y, jnp.stack(ssm_re), jnp.stack(ssm_im), ckv, kr

    gn = n_groups * n_state
    h0_re = state_ssm_re.reshape(n_a, bd, gn)
    h0_im = state_ssm_im.reshape(n_a, bd, gn)
    y_s, re_s, im_s, ckv_s, kr_s = run(x_sample.reshape(1, bd, d), 1, h0_re, h0_im)
    y_p, re_p, im_p, ckv_p, kr_p = run(x_prompt, 0, None, None)
    return (y_p, y_s.reshape(bd, dec_seq, d), re_p, im_p, ckv_p, kr_p,
            re_s, im_s, ckv_s.reshape(bd, dec_seq, -1), kr_s.reshape(bd, dec_seq, -1))
```

```python
import functools
import math

import jax
import jax.numpy as jnp
from jax import lax
from jax.experimental import pallas as pl
from jax.experimental.pallas import tpu as pltpu

F32 = jnp.float32
BF16 = jnp.bfloat16
EPS = 1e-6
ROPE_THETA = 10000.0
PEER_TOPK = 16
NEG_INF = -1e30
LANES = 128
PEER_GATE_W = 512
FLASH_HEADS_PER_STEP = 4
PEER_ROUTE_W = 256
PEER_SUB_CHUNK = 1024
PEER_SUBS_PER_STEP = 2
DECODE_PAGES_PER_CHUNK = 64
DECODE_CHAINS = 1
SSM_GROUPS_PER_BLOCK = 16
VMEM_LIMIT = 56 * 1024 * 1024

_NN = (((1,), (0,)), ((), ()))
_NT = (((1,), (1,)), ((), ()))


def _dot(a, b, dims=_NN):
    return lax.dot_general(a, b, dims, preferred_element_type=F32)


def _split(a):
    hi = a.astype(BF16)
    lo = (a - hi.astype(F32)).astype(BF16)
    return hi, lo


def _dot3(a, b, dims=_NN):
    ah, al = _split(a)
    bh, bl = _split(b)
    return _dot(ah, bh, dims) + (_dot(ah, bl, dims) + _dot(al, bh, dims))


def _dot3w(a, wh, wl, dims=_NN):
    ah, al = _split(a)
    return _dot(ah, wh, dims) + (_dot(ah, wl, dims) + _dot(al, wh, dims))


def _sigmoid(x):
    return 1.0 / (1.0 + jnp.exp(-x))


def _rms(x, g):
    return x * lax.rsqrt(jnp.mean(x * x, axis=-1, keepdims=True) + EPS) * g


def _norm_mod(x, g, sh, sc):
    return _rms(x, g) * (1.0 + sc) + sh


def _params(*sem, flags=None):
    return pltpu.CompilerParams(dimension_semantics=sem, vmem_limit_bytes=VMEM_LIMIT, flags=flags)


def _full(a):
    nd = a.ndim
    return pl.BlockSpec(a.shape, lambda *_: (0,) * nd)


def _tok(width, tb):
    return pl.BlockSpec((1, tb, width), lambda b, i, *_: (b, i, 0))


def _modspec(m, tb):
    if m.shape[1] == 1:
        return pl.BlockSpec((1, 1, m.shape[2]), lambda b, i, *_: (b, 0, 0))
    return pl.BlockSpec((1, tb, m.shape[2]), lambda b, i, *_: (b, i, 0))


def _tok_block(lg, cap=512):
    tb = min(lg, cap)
    assert lg % tb == 0
    return tb


def _hi_lo(w):
    hi = lax.bitcast_convert_type(lax.bitcast_convert_type(w, jnp.uint32) & jnp.uint32(0xFFFF0000), F32)
    return hi.astype(BF16), (w - hi).astype(BF16)


def _mod_kernel(c_ref, wh_ref, wl_ref, b_ref, o_ref):
    c = c_ref[...]
    o_ref[...] = _dot3w(c * _sigmoid(c), wh_ref[...], wl_ref[...]) + b_ref[...]


def _mod_linear(c, w, b):
    bc, d = c.shape
    n = w.shape[1]
    tn = min(n, 1024)
    assert n % tn == 0
    wh, wl = _hi_lo(w)
    return pl.pallas_call(
        _mod_kernel,
        grid=(n // tn,),
        in_specs=[pl.BlockSpec((bc, d), lambda j: (0, 0)),
                  pl.BlockSpec((d, tn), lambda j: (0, j)),
                  pl.BlockSpec((d, tn), lambda j: (0, j)),
                  pl.BlockSpec((1, tn), lambda j: (0, j))],
        out_specs=pl.BlockSpec((bc, tn), lambda j: (0, j)),
        out_shape=jax.ShapeDtypeStruct((bc, n), F32),
        compiler_params=_params("parallel"),
        name="mod_linear",
    )(c, wh, wl, b.reshape(1, n))


def _s5_prep(a_re, a_im, log_dt, b_re, b_im, c_re, c_im):
    g, n, p = b_re.shape
    gb = min(g, SSM_GROUPS_PER_BLOCK)
    assert g % gb == 0
    k = g // gb
    dt = jnp.exp(log_dt.astype(F32))[:, None]
    a_re = a_re.astype(F32)
    a_im = a_im.astype(F32)
    er = jnp.exp(a_re * dt)
    lr = er * jnp.cos(a_im * dt)
    li = er * jnp.sin(a_im * dt)
    den = a_re * a_re + a_im * a_im
    kr = ((lr - 1.0) * a_re + li * a_im) / den
    ki = (li * a_re - (lr - 1.0) * a_im) / den
    bbr = kr[..., None] * b_re - ki[..., None] * b_im
    bbi = kr[..., None] * b_im + ki[..., None] * b_re
    eye = jnp.eye(gb, dtype=F32)

    def blk_in(b):
        bt = b.reshape(k, gb, n, p).transpose(0, 1, 3, 2)
        return (bt[:, :, :, None, :] * eye[None, :, None, :, None]).reshape(k, gb * p, gb * n)

    def blk_out(c):
        ct = c.reshape(k, gb, p, n).transpose(0, 1, 3, 2)
        return (ct[:, :, :, None, :] * eye[None, :, None, :, None]).reshape(k, gb * n, gb * p)

    b_blk = jnp.concatenate([blk_in(bbr), blk_in(bbi)], axis=-1)
    c_blk = jnp.concatenate([blk_out(c_re.astype(F32)), -blk_out(c_im.astype(F32))], axis=1)
    bh, bl = _hi_lo(b_blk)
    return dict(bh=bh, bl=bl, c=c_blk.astype(BF16), lam_re=lr.reshape(1, g * n), lam_im=li.reshape(1, g * n))


def _s5_input(h, bh_ref, bl_ref, k, w):
    hs = h[:, k * w:(k + 1) * w]
    return _dot3w(hs, bh_ref[k], bl_ref[k])


def _s5_prompt_kernel(x_ref, g_ref, sh_ref, sc_ref, bh_ref, bl_ref, c_ref, lre_ref, lim_ref, d_ref,
                      z_ref, sre_ref, sim_ref, s_re, s_im, st_re, st_im, h_ref, hh_ref, hl_ref,
                      *, nblk, tc, pitch, kper):
    j = pl.program_id(0)

    @pl.when(j == 0)
    def _():
        st_re[...] = jnp.zeros_like(st_re)
        st_im[...] = jnp.zeros_like(st_im)

    bsz, _, d = x_ref.shape
    w = d // nblk
    nw = lre_ref.shape[2]
    n_slabs = nw // LANES
    rows = kper * bsz
    for b in range(bsz):
        h = _norm_mod(x_ref[b], g_ref[...], sh_ref[b], sc_ref[b])
        h_ref[b] = h
        hh, hl = _split(h)
        hh_ref[b * tc:(b + 1) * tc, :] = hh
        hl_ref[b * tc:(b + 1) * tc, :] = hl

    for pair in range(nblk // kper):
        for kk in range(kper):
            k = pair * kper + kk
            hh = hh_ref[:, k * w:(k + 1) * w]
            hl = hl_ref[:, k * w:(k + 1) * w]
            bu = _dot(hh, bh_ref[k]) + (_dot(hh, bl_ref[k]) + _dot(hl, bh_ref[k]))
            for b in range(bsz):
                r0 = (kk * bsz + b) * pitch
                for sl in range(n_slabs):
                    s_re[sl, r0:r0 + tc, :] = bu[b * tc:(b + 1) * tc, sl * LANES:(sl + 1) * LANES]
                    s_im[sl, r0:r0 + tc, :] = bu[b * tc:(b + 1) * tc, nw + sl * LANES:nw + (sl + 1) * LANES]
        lr = lre_ref[pair]
        li = lim_ref[pair]

        def step(t, carry):
            sr, si = carry
            idx = pl.ds(t, rows, stride=pitch)
            br = jnp.concatenate([s_re[sl, idx, :] for sl in range(n_slabs)], axis=1)
            bi = jnp.concatenate([s_im[sl, idx, :] for sl in range(n_slabs)], axis=1)
            nr = lr * sr - li * si + br
            ni = lr * si + li * sr + bi
            for sl in range(n_slabs):
                s_re[sl, idx, :] = nr[:, sl * LANES:(sl + 1) * LANES]
                s_im[sl, idx, :] = ni[:, sl * LANES:(sl + 1) * LANES]
            return nr, ni

        sr, si = lax.fori_loop(0, tc, step, (st_re[pair], st_im[pair]))
        st_re[pair] = sr
        st_im[pair] = si
        for kk in range(kper):
            k = pair * kper + kk
            for b in range(bsz):
                r0 = (kk * bsz + b) * pitch
                hre = jnp.concatenate([s_re[sl, r0:r0 + tc, :] for sl in range(n_slabs)], axis=1)
                him = jnp.concatenate([s_im[sl, r0:r0 + tc, :] for sl in range(n_slabs)], axis=1)
                y = _dot(hre.astype(BF16), c_ref[k, :nw, :]) + _dot(him.astype(BF16), c_ref[k, nw:, :])
                hk = h_ref[b, :, k * w:(k + 1) * w]
                z_ref[b, :, k * w:(k + 1) * w] = jax.nn.gelu(y + d_ref[:, k * w:(k + 1) * w] * hk)
    sre_ref[...] = st_re[...]
    sim_ref[...] = st_im[...]


def _s5_prompt(x, g, sh, sc, prm, d_skip):
    bsz, lg, d = x.shape
    nblk = prm["bh"].shape[0]
    gn = prm["lam_re"].shape[1]
    nw = gn // nblk
    sublanes = 8
    assert sublanes % bsz == 0 and nblk % (sublanes // bsz) == 0 and nw % LANES == 0
    kper = sublanes // bsz
    npairs = nblk // kper
    tc = _tok_block(lg, 128)
    pitch = tc + 8 if (tc // 8) % 2 == 0 else tc
    rows = kper * bsz

    def pack_lam(v):
        return jnp.repeat(v.reshape(npairs, kper, nw), bsz, axis=1)

    lre, lim = pack_lam(prm["lam_re"]), pack_lam(prm["lam_im"])
    kern = functools.partial(_s5_prompt_kernel, nblk=nblk, tc=tc, pitch=pitch, kper=kper)
    blk = pl.BlockSpec((bsz, tc, d), lambda j: (0, j, 0))
    st_spec = pl.BlockSpec((npairs, rows, nw), lambda j: (0, 0, 0))
    st_shape = jax.ShapeDtypeStruct((npairs, rows, nw), F32)
    z, sre, sim = pl.pallas_call(
        kern,
        grid=(lg // tc,),
        in_specs=[blk, _full(g), _full(sh), _full(sc),
                  _full(prm["bh"]), _full(prm["bl"]), _full(prm["c"]), _full(lre), _full(lim), _full(d_skip)],
        out_specs=[blk, st_spec, st_spec],
        out_shape=[jax.ShapeDtypeStruct((bsz, lg, d), F32), st_shape, st_shape],
        scratch_shapes=[pltpu.VMEM((nw // LANES, rows * pitch, LANES), F32),
                        pltpu.VMEM((nw // LANES, rows * pitch, LANES), F32),
                        pltpu.VMEM((npairs, rows, nw), F32), pltpu.VMEM((npairs, rows, nw), F32),
                        pltpu.VMEM((bsz, tc, d), F32), pltpu.VMEM((bsz * tc, d), BF16),
                        pltpu.VMEM((bsz * tc, d), BF16)],
        compiler_params=_params("arbitrary"),
        name="s5_prompt",
    )(x, g, sh, sc, prm["bh"], prm["bl"], prm["c"], lre, lim, d_skip)

    def unpack(s):
        return s.reshape(npairs, kper, bsz, nw).transpose(2, 0, 1, 3).reshape(bsz, gn)

    return z, unpack(sre), unpack(sim)


def _s5_step_kernel(x_ref, g_ref, sh_ref, sc_ref, h0re_ref, h0im_ref, bh_ref, bl_ref, c_ref,
                    lre_ref, lim_ref, d_ref, z_ref, sre_ref, sim_ref, *, nblk):
    h = _norm_mod(x_ref[0], g_ref[...], sh_ref[0], sc_ref[0])
    w = h.shape[1] // nblk
    nw = lre_ref.shape[1] // nblk
    for k in range(nblk):
        cols = slice(k * nw, (k + 1) * nw)
        bu = _s5_input(h, bh_ref, bl_ref, k, w)
        lr = lre_ref[:, cols]
        li = lim_ref[:, cols]
        pr = h0re_ref[:, cols]
        pi = h0im_ref[:, cols]
        nr = lr * pr - li * pi + bu[:, :nw]
        ni = lr * pi + li * pr + bu[:, nw:]
        sre_ref[:, cols] = nr
        sim_ref[:, cols] = ni
        y = _dot(nr.astype(BF16), c_ref[k, :nw, :]) + _dot(ni.astype(BF16), c_ref[k, nw:, :])
        hk = h[:, k * w:(k + 1) * w]
        z_ref[0, :, k * w:(k + 1) * w] = jax.nn.gelu(y + d_ref[:, k * w:(k + 1) * w] * hk)


def _s5_step(x, g, sh, sc, h0_re, h0_im, prm, d_skip):
    _, rows, d = x.shape
    nblk = prm["bh"].shape[0]
    gn = prm["lam_re"].shape[1]
    tb = _tok_block(rows, 128)
    st_spec = pl.BlockSpec((tb, gn), lambda b, i: (i, 0))
    return pl.pallas_call(
        functools.partial(_s5_step_kernel, nblk=nblk),
        grid=(1, rows // tb),
        in_specs=[_tok(d, tb), _full(g), _modspec(sh, tb), _modspec(sc, tb), st_spec, st_spec,
                  _full(prm["bh"]), _full(prm["bl"]), _full(prm["c"]),
                  _full(prm["lam_re"]), _full(prm["lam_im"]), _full(d_skip)],
        out_specs=[_tok(d, tb), st_spec, st_spec],
        out_shape=[jax.ShapeDtypeStruct((1, rows, d), F32),
                   jax.ShapeDtypeStruct((rows, gn), F32),
                   jax.ShapeDtypeStruct((rows, gn), F32)],
        compiler_params=_params("parallel", "parallel"),
        name="s5_step",
    )(x, g, sh, sc, h0_re, h0_im, prm["bh"], prm["bl"], prm["c"], prm["lam_re"], prm["lam_im"], d_skip)


def _glu_res_kernel(z_ref, w_ref, x_ref, ga_ref, o_ref):
    y = _dot(z_ref[0].astype(BF16), w_ref[...])
    d = x_ref.shape[2]
    o_ref[0] = x_ref[0] + ga_ref[0] * (y[:, :d] * _sigmoid(y[:, d:]))


def _glu_residual(z, w_glu_bf, x, ga):
    bg, lg, d = x.shape
    tb = _tok_block(lg)
    return pl.pallas_call(
        _glu_res_kernel,
        grid=(bg, lg // tb),
        in_specs=[_tok(d, tb), _full(w_glu_bf), _tok(d, tb), _modspec(ga, tb)],
        out_specs=_tok(d, tb),
        out_shape=jax.ShapeDtypeStruct((bg, lg, d), F32),
        compiler_params=_params("parallel", "parallel"),
        name="glu_residual",
    )(z, w_glu_bf, x, ga)


def _cand_pairs(topk):
    return [(a, b) for a in range(topk + 1) for b in range(topk + 1) if (a + 1) * (b + 1) <= topk + 1]


def _top_rows(s, count, rank_count=0):
    out = []
    rank = jnp.zeros(s.shape, F32) if rank_count else None
    for r in range(count):
        if r == 0:
            cur = s
        else:
            below = s < out[-1]
            cur = jnp.where(below, s, NEG_INF)
            if r <= rank_count:
                rank = jnp.where(below, float(r), rank)
        out.append(jnp.max(cur, axis=0, keepdims=True))
    return out, rank


def _peer_route_kernel(x_ref, g_ref, sh_ref, sc_ref, wqh_ref, wql_ref, keys_ref,
                       hb_ref, r1_ref, e1_ref, nb_ref, coef_ref, cand_ref, *, topk):
    h = _norm_mod(x_ref[0], g_ref[...], sh_ref[0], sc_ref[0])
    hb_ref[0] = h.astype(BF16)
    q = _dot3w(h, wqh_ref[...], wql_ref[...])
    n_heads, _, n_keys, dk = keys_ref.shape
    pairs = _cand_pairs(topk)
    cand_ref[...] = jnp.full(cand_ref.shape, NEG_INF, F32)
    tb = q.shape[0]
    gw = r1_ref.shape[4]
    rw = cand_ref.shape[1]
    for hd, t0 in [(hd, t0) for hd in range(n_heads) for t0 in range(0, tb, rw)]:
        st, tops, rank1 = [], [], None
        for s in range(2):
            qs = q[t0:t0 + rw, (hd * 2 + s) * dk:(hd * 2 + s + 1) * dk]
            sc = _dot3(keys_ref[hd, s], qs, _NT)
            st.append(sc)
            top, rank = _top_rows(sc, topk + 1, rank_count=topk * s)
            tops.append(top)
            rank1 = rank
        for r, (a, b) in enumerate(pairs):
            cand_ref[pl.ds(r, 1), :] = tops[0][a] + tops[1][b]
        cand = cand_ref[...]
        best, _ = _top_rows(cand, topk + 1)
        tau = 0.5 * (best[topk - 1] + best[topk])
        mtot = tops[0][0] + tops[1][0]
        z = jnp.sum(jnp.where(cand >= tau, jnp.exp(cand - mtot), 0.0), axis=0, keepdims=True)
        nb = jnp.zeros(st[0].shape, F32)
        for b in range(topk):
            nb = jnp.where(st[0] >= tau - tops[1][b], float(b + 1), nb)
        outs = ((r1_ref, rank1.astype(BF16)), (e1_ref, jnp.exp(st[1] - tops[1][0]).astype(BF16)),
                (nb_ref, nb), (coef_ref, jnp.exp(st[0] - tops[0][0]) / z))
        for ref, val in outs:
            val = pltpu.bitcast(val, jnp.uint32) if val.dtype == BF16 else val
            if rw >= gw:
                for k in range(rw // gw):
                    ref[0, hd, t0 // gw + k] = val[:, k * gw:(k + 1) * gw]
            else:
                ref[0, hd, t0 // gw, :, t0 % gw:t0 % gw + rw] = val


def _peer_route(x, g, sh, sc, wqh, wql, keys, topk=PEER_TOPK):
    bg, lg, d = x.shape
    n_heads, _, n_keys, _ = keys.shape
    tb = _tok_block(lg)
    n_cand = -(-len(_cand_pairs(topk)) // 8) * 8
    gw = min(tb, PEER_GATE_W)

    def gate(rows):
        return pl.BlockSpec((1, n_heads, tb // gw, rows, gw), lambda b, i: (b, 0, i, 0, 0))

    def gate_shape(rows, dt):
        return jax.ShapeDtypeStruct((bg, n_heads, lg // gw, rows, gw), dt)

    packed = n_keys // 2
    gate_shapes = [gate_shape(packed, jnp.uint32), gate_shape(packed, jnp.uint32),
                   gate_shape(n_keys, F32), gate_shape(n_keys, F32)]
    return pl.pallas_call(
        functools.partial(_peer_route_kernel, topk=topk),
        grid=(bg, lg // tb),
        in_specs=[_tok(d, tb), _full(g), _modspec(sh, tb), _modspec(sc, tb),
                  _full(wqh), _full(wql), _full(keys)],
        out_specs=[_tok(d, tb), gate(packed), gate(packed), gate(n_keys), gate(n_keys)],
        out_shape=[jax.ShapeDtypeStruct((bg, lg, d), BF16)] + gate_shapes,
        scratch_shapes=[pltpu.VMEM((n_cand, min(tb, PEER_ROUTE_W)), F32)],
        compiler_params=_params("parallel", "parallel"),
        name="peer_route",
    )(x, g, sh, sc, wqh, wql, keys)


def _peer_dense_kernel(hb_ref, r1_ref, e1_ref, nb_ref, coef_ref, u_ref, vt_ref, x_ref, ga_ref,
                       o_ref, acc_ref, *gate_refs, n_keys):
    c = pl.program_id(2)
    n_steps = pl.num_programs(2)
    n_heads = r1_ref.shape[1]
    gw = r1_ref.shape[4]
    tb = hb_ref.shape[1]
    n_sub = len(gate_refs)
    ec = u_ref.shape[0] // n_sub
    per = ec // n_keys

    def build_gate(i0, n_i, dst):
        for ii in range(n_i):
            for lt in range(tb // gw):
                gate = None
                for hd in range(n_heads):
                    nb = nb_ref[0, hd, lt, pl.ds(i0 + ii, 1), :].astype(BF16)
                    coef = coef_ref[0, hd, lt, pl.ds(i0 + ii, 1), :].astype(BF16)
                    r1 = pltpu.bitcast(r1_ref[0, hd, lt], BF16)
                    e1 = pltpu.bitcast(e1_ref[0, hd, lt], BF16)
                    term = jnp.where(r1 < nb, e1 * coef, jnp.zeros((), BF16))
                    gate = term if gate is None else gate + term
                dst[ii * n_keys:(ii + 1) * n_keys, lt * gw:(lt + 1) * gw] = gate

    @pl.when(c == 0)
    def _():
        acc_ref[...] = jnp.zeros_like(acc_ref)

    acts = []
    for s in range(n_sub):
        rows = slice(s * ec, (s + 1) * ec)
        acts.append(jax.nn.gelu(_dot(u_ref[rows, :], hb_ref[0], _NT).astype(BF16)))
    for s in range(n_sub):
        rows = slice(s * ec, (s + 1) * ec)
        build_gate((c * n_sub + s) * per, per, gate_refs[s])
        acc_ref[...] += _dot(vt_ref[:, rows], acts[s] * gate_refs[s][...])

    @pl.when(c == n_steps - 1)
    def _():
        o_ref[0] = x_ref[0] + ga_ref[0] * acc_ref[...].T


def _peer_dense(hb, gates, u_bf, vt_bf, x, ga):
    bg, lg, d = x.shape
    n_heads, n_keys = gates[2].shape[1], gates[2].shape[3]
    n_exp = u_bf.shape[0]
    tb = _tok_block(lg)
    sub = min(n_exp, PEER_SUB_CHUNK)
    n_sub = math.gcd(n_exp // sub, PEER_SUBS_PER_STEP)
    ec = n_sub * sub
    assert n_exp % ec == 0 and sub % n_keys == 0

    def gate(a):
        gw = a.shape[4]
        return pl.BlockSpec((1, n_heads, tb // gw, a.shape[3], gw), lambda b, i, c: (b, 0, i, 0, 0))

    return pl.pallas_call(
        functools.partial(_peer_dense_kernel, n_keys=n_keys),
        grid=(bg, lg // tb, n_exp // ec),
        in_specs=[_tok(d, tb)] + [gate(a) for a in gates] + [
                  pl.BlockSpec((ec, d), lambda b, i, c: (c, 0)),
                  pl.BlockSpec((d, ec), lambda b, i, c: (0, c)),
                  _tok(d, tb), _modspec(ga, tb)],
        out_specs=_tok(d, tb),
        out_shape=jax.ShapeDtypeStruct((bg, lg, d), F32),
        scratch_shapes=[pltpu.VMEM((d, tb), F32)] + [pltpu.VMEM((sub, tb), BF16)] * n_sub,
        compiler_params=_params("parallel", "parallel", "arbitrary"),
        name="peer_dense",
    )(hb, *gates, u_bf, vt_bf, x, ga)


def _peer(x, g, sh, sc, ga, pw):
    hb, *gates = _peer_route(x, g, sh, sc, pw["wqh"], pw["wql"], pw["keys"])
    return _peer_dense(hb, gates, pw["u"], pw["vt"], x, ga)


def _rope_tables(positions, rope_dim, scale=1.0):
    inv = ROPE_THETA ** (-jnp.arange(0, rope_dim, 2, dtype=F32) / rope_dim)
    ang = positions.astype(F32)[:, None] * inv
    cos, sin = jnp.cos(ang), jnp.sin(ang)
    return jnp.concatenate([cos, cos], -1) * scale, jnp.concatenate([sin, sin], -1) * scale


def _rot_cols(w):
    half = w.shape[-1] // 2
    return jnp.concatenate([-w[..., half:], w[..., :half]], axis=-1)


def _kv_kernel(x_ref, g_ref, sh_ref, sc_ref, wdkv_ref, gckv_ref, wkr_ref, wkrr_ref, cos_ref, sin_ref,
               *rest, heads_out):
    h = _norm_mod(x_ref[0], g_ref[...], sh_ref[0], sc_ref[0])
    ckv = _rms(_dot3(h, wdkv_ref[...]), gckv_ref[...])
    kr = _dot3(h, wkr_ref[...]) * cos_ref[...] + _dot3(h, wkrr_ref[...]) * sin_ref[...]
    if heads_out:
        wk_ref, pk_ref, wv_ref, ckv_ref, kr_ref, kcat_ref, v_ref = rest
        cb = ckv.astype(BF16)
        kcat_ref[0] = (_dot(cb, wk_ref[...]) + _dot(kr.astype(BF16), pk_ref[...])).astype(BF16)
        v_ref[0] = _dot(cb, wv_ref[...]).astype(BF16)
    else:
        ckv_ref, kr_ref = rest
    ckv_ref[0] = ckv
    kr_ref[0] = kr


def _kv_proj(x, g, sh, sc, aw, cos, sin, heads_out):
    bg, lg, d = x.shape
    tb = _tok_block(lg)
    c_dim, r_dim = aw["w_dkv"].shape[1], aw["w_kr"].shape[1]
    tab = _full(cos) if cos.shape[0] == 1 else pl.BlockSpec((tb, r_dim), lambda b, i: (i, 0))
    ins = [x, g, sh, sc, aw["w_dkv"], aw["g_ckv"], aw["w_kr"], aw["w_kr_rot"], cos, sin]
    in_specs = [_tok(d, tb), _full(g), _modspec(sh, tb), _modspec(sc, tb), _full(aw["w_dkv"]),
                _full(aw["g_ckv"]), _full(aw["w_kr"]), _full(aw["w_kr_rot"]), tab, tab]
    out_specs = [_tok(c_dim, tb), _tok(r_dim, tb)]
    out_shape = [jax.ShapeDtypeStruct((bg, lg, c_dim), F32), jax.ShapeDtypeStruct((bg, lg, r_dim), F32)]
    if heads_out:
        hw = aw["wk_all"].shape[1]
        ins += [aw["wk_all"], aw["pk"], aw["wv_all"]]
        in_specs += [_full(aw["wk_all"]), _full(aw["pk"]), _full(aw["wv_all"])]
        out_specs += [_tok(hw, tb), _tok(hw, tb)]
        out_shape += [jax.ShapeDtypeStruct((bg, lg, hw), BF16)] * 2
    return pl.pallas_call(
        functools.partial(_kv_kernel, heads_out=heads_out),
        grid=(bg, lg // tb),
        in_specs=in_specs, out_specs=out_specs, out_shape=out_shape,
        compiler_params=_params("parallel", "parallel"),
        name="kv_proj",
    )(*ins)


def _q_kernel(x_ref, g_ref, sh_ref, sc_ref, wdq_ref, gcq_ref, wqa_ref, wqb_ref, cos_ref, sin_ref, q_ref,
              *, n_heads):
    h = _norm_mod(x_ref[0], g_ref[...], sh_ref[0], sc_ref[0])
    cq = _rms(_dot(h.astype(BF16), wdq_ref[...]), gcq_ref[...]).astype(BF16)
    qa = _dot(cq, wqa_ref[...])
    qb = _dot(cq, wqb_ref[...])
    cos = cos_ref[...]
    sin = sin_ref[...]
    for hd in range(n_heads):
        cols = slice(hd * LANES, (hd + 1) * LANES)
        q_ref[0, :, cols] = (qa[:, cols] * cos + qb[:, cols] * sin).astype(q_ref.dtype)


def _q_proj(x, g, sh, sc, lw, cos, sin, n_heads, out_dtype):
    bg, lg, d = x.shape
    tb = _tok_block(lg)
    hw = n_heads * LANES
    tab = _full(cos) if cos.shape[0] == 1 else pl.BlockSpec((tb, LANES), lambda b, i: (i, 0))
    return pl.pallas_call(
        functools.partial(_q_kernel, n_heads=n_heads),
        grid=(bg, lg // tb),
        in_specs=[_tok(d, tb), _full(g), _modspec(sh, tb), _modspec(sc, tb), _full(lw["w_dq"]),
                  _full(lw["g_cq"]), _full(lw["wq_a"]), _full(lw["wq_b"]), tab, tab],
        out_specs=_tok(hw, tb),
        out_shape=jax.ShapeDtypeStruct((bg, lg, hw), out_dtype),
        compiler_params=_params("parallel", "parallel"),
        name="q_proj",
    )(x, g, sh, sc, lw["w_dq"], lw["g_cq"], lw["wq_a"], lw["wq_b"], cos, sin)


def _flash_kernel(q_ref, k_ref, v_ref, o_ref, m_ref, l_ref, acc_ref, *, tq, hps):
    qi = pl.program_id(2)
    m_ref[...] = jnp.full(m_ref.shape, NEG_INF, F32)
    l_ref[...] = jnp.zeros_like(l_ref)
    acc_ref[...] = jnp.zeros_like(acc_ref)

    def update(j, masked):
        start = pl.multiple_of(j * tq, tq)
        for hd in range(hps):
            cols = slice(hd * LANES, (hd + 1) * LANES)
            k = k_ref[0, pl.ds(start, tq), cols]
            v = v_ref[0, pl.ds(start, tq), cols]
            s = _dot(q_ref[0, :, cols], k, _NT)
            if masked:
                row = lax.broadcasted_iota(jnp.int32, s.shape, 0)
                col = lax.broadcasted_iota(jnp.int32, s.shape, 1)
                s = jnp.where(col <= row, s, NEG_INF)
            m_old = m_ref[hd]
            m_new = jnp.maximum(m_old, jnp.max(s, axis=-1, keepdims=True))
            corr = jnp.exp2(m_old - m_new)
            p = jnp.exp2(s - jnp.concatenate([m_new] * (tq // LANES), axis=1))
            l_ref[hd] = l_ref[hd] * corr + jnp.sum(p, axis=-1, keepdims=True)
            acc_ref[hd] = acc_ref[hd] * corr + _dot(p.astype(BF16), v)
            m_ref[hd] = m_new

    def body(j, carry):
        update(j, False)
        return carry

    lax.fori_loop(0, qi, body, 0)
    update(qi, True)
    for hd in range(hps):
        o_ref[0, :, hd * LANES:(hd + 1) * LANES] = (acc_ref[hd] / l_ref[hd]).astype(o_ref.dtype)


def _flash_attention(q, k, v, n_heads):
    bsz, lg, _ = q.shape
    tq = _tok_block(lg)
    hps = math.gcd(n_heads, FLASH_HEADS_PER_STEP)
    seq = pl.BlockSpec((1, lg, hps * LANES), lambda b, h, i: (b, 0, h))
    blk = pl.BlockSpec((1, tq, hps * LANES), lambda b, h, i: (b, i, h))
    stat = pltpu.VMEM((hps, tq, LANES), F32)
    return pl.pallas_call(
        functools.partial(_flash_kernel, tq=tq, hps=hps),
        grid=(bsz, n_heads // hps, lg // tq),
        in_specs=[blk, seq, seq],
        out_specs=blk,
        out_shape=jax.ShapeDtypeStruct(q.shape, BF16),
        scratch_shapes=[stat, stat, stat],
        compiler_params=_params("parallel", "parallel", "arbitrary"),
        name="flash_attention",
    )(q, k, v)


def _absorb_kernel(q_ref, wk_ref, qlat_ref, qpe_ref, *, n_heads, nope, rope):
    c_dim = wk_ref.shape[0]
    for hd in range(n_heads):
        qh = q_ref[:, hd * LANES:(hd + 1) * LANES]
        wk = wk_ref[:, hd * LANES:(hd + 1) * LANES]
        qlat_ref[:, hd * c_dim:(hd + 1) * c_dim] = _dot(qh.astype(BF16), wk, _NT)
        qpe_ref[:, hd * rope:(hd + 1) * rope] = qh[:, nope:nope + rope]


def _absorb(q, wk_all, n_heads, nope, rope):
    rows = q.shape[0]
    c_dim = wk_all.shape[0]
    return pl.pallas_call(
        functools.partial(_absorb_kernel, n_heads=n_heads, nope=nope, rope=rope),
        in_specs=[_full(q), _full(wk_all)],
        out_specs=[pl.BlockSpec((rows, n_heads * c_dim), lambda: (0, 0)),
                   pl.BlockSpec((rows, n_heads * rope), lambda: (0, 0))],
        out_shape=[jax.ShapeDtypeStruct((rows, n_heads * c_dim), F32),
                   jax.ShapeDtypeStruct((rows, n_heads * rope), F32)],
        compiler_params=pltpu.CompilerParams(vmem_limit_bytes=VMEM_LIMIT),
        name="q_absorb",
    )(q, wk_all)


def _decode_kernel(pt_ref, qlat_ref, qpe_ref, cnew_ref, rnew_ref, cc_hbm, cr_hbm, o_ref,
                   kc_buf, kr_buf, sem_c, sem_r, *, pages, n_chunks, chains):
    b = pl.program_id(0)
    n_rows = pl.num_programs(0)

    def chunk_copies(row, chunk, slot, src_page=None):
        cps = []
        for j in range(pages):
            pg = pt_ref[row, chunk * pages + j] if src_page is None else src_page
            cps.append(pltpu.make_async_copy(cc_hbm.at[pg], kc_buf.at[slot, j], sem_c.at[slot]))
            cps.append(pltpu.make_async_copy(cr_hbm.at[pg], kr_buf.at[slot, j], sem_r.at[slot]))
        return cps

    def start_chunk(row, chunk, slot):
        for cp in chunk_copies(row, chunk, slot):
            cp.start()

    def wait_chunk(slot):
        for cp in chunk_copies(0, 0, slot, src_page=0):
            cp.wait()

    @pl.when(b == 0)
    def _():
        start_chunk(0, 0, 0)

    ql = qlat_ref[0]
    qp = qpe_ref[0]
    qlb = ql.astype(BF16)
    qpb = qp.astype(BF16)
    per = pages // chains
    n_heads, c_dim = ql.shape
    state = [(jnp.full((n_heads, 1), NEG_INF, F32), jnp.zeros((n_heads, 1), F32),
              jnp.zeros((n_heads, c_dim), F32)) for _ in range(chains)]
    for k in range(n_chunks):
        slot = k % 2
        if k + 1 < n_chunks:
            start_chunk(b, k + 1, 1 - slot)
        else:
            @pl.when(b + 1 < n_rows)
            def _():
                start_chunk(b + 1, 0, 1 - slot)
        wait_chunk(slot)
        for ch in range(chains):
            js = range(ch * per, (ch + 1) * per)
            kc = jnp.concatenate([kc_buf[slot, j].astype(BF16) for j in js], axis=0)
            kr = jnp.concatenate([kr_buf[slot, j].astype(BF16) for j in js], axis=1)
            s = _dot(qlb, kc, _NT) + _dot(qpb, kr)
            m_old, l_old, acc = state[ch]
            m_new = jnp.maximum(m_old, jnp.max(s, axis=-1, keepdims=True))
            corr = jnp.exp(m_old - m_new)
            p = jnp.exp(s - m_new)
            state[ch] = (m_new, l_old * corr + jnp.sum(p, axis=-1, keepdims=True),
                         acc * corr + _dot(p.astype(BF16), kc))

    cn = cnew_ref[0]
    rn = rnew_ref[0]
    s_new = jnp.sum(ql * cn, axis=-1, keepdims=True) + jnp.sum(qp * rn, axis=-1, keepdims=True)
    m_fin = s_new
    for m, _, _ in state:
        m_fin = jnp.maximum(m_fin, m)
    p_new = jnp.exp(s_new - m_fin)
    l_fin = p_new
    acc_fin = p_new * cn
    for m, l, acc in state:
        w = jnp.exp(m - m_fin)
        l_fin = l_fin + l * w
        acc_fin = acc_fin + acc * w
    o_ref[0] = acc_fin / l_fin


def _decode_attention(qlat, qpe, ckv_new, kr_new, cache_c, cache_rt, page_table):
    bd, n_heads, c_dim = qlat.shape
    r_dim = qpe.shape[2]
    page = cache_c.shape[1]
    n_pages = page_table.shape[1]
    pages = max(p for p in range(1, DECODE_PAGES_PER_CHUNK + 1)
                if n_pages % p == 0 and (n_pages // p) % 2 == 0)
    chains = math.gcd(pages, DECODE_CHAINS)
    per_b3 = lambda w: pl.BlockSpec((1, n_heads, w), lambda b, pt: (b, 0, 0))
    new3 = lambda w: pl.BlockSpec((1, 1, w), lambda b, pt: (b, 0, 0))
    grid_spec = pltpu.PrefetchScalarGridSpec(
        num_scalar_prefetch=1,
        grid=(bd,),
        in_specs=[per_b3(c_dim), per_b3(r_dim), new3(c_dim), new3(r_dim),
                  pl.BlockSpec(memory_space=pl.ANY), pl.BlockSpec(memory_space=pl.ANY)],
        out_specs=per_b3(c_dim),
        scratch_shapes=[pltpu.VMEM((2, pages, page, c_dim), F32), pltpu.VMEM((2, pages, r_dim, page), F32),
                        pltpu.SemaphoreType.DMA((2,)), pltpu.SemaphoreType.DMA((2,))],
    )
    return pl.pallas_call(
        functools.partial(_decode_kernel, pages=pages, n_chunks=n_pages // pages, chains=chains),
        grid_spec=grid_spec,
        out_shape=jax.ShapeDtypeStruct((bd, n_heads, c_dim), F32),
        compiler_params=_params("arbitrary"),
        name="decode_attention",
    )(page_table, qlat, qpe, ckv_new, kr_new, cache_c, cache_rt)


def _upproj_kernel(ol_ref, wv_ref, o_ref, *, n_heads):
    c_dim = wv_ref.shape[0]
    for hd in range(n_heads):
        ol = ol_ref[:, hd * c_dim:(hd + 1) * c_dim].astype(BF16)
        o_ref[0, :, hd * LANES:(hd + 1) * LANES] = _dot(ol, wv_ref[:, hd * LANES:(hd + 1) * LANES]).astype(BF16)


def _upproj(o_lat, wv_all, n_heads):
    rows = o_lat.shape[0]
    hw = wv_all.shape[1]
    return pl.pallas_call(
        functools.partial(_upproj_kernel, n_heads=n_heads),
        in_specs=[_full(o_lat), _full(wv_all)],
        out_specs=pl.BlockSpec((1, rows, hw), lambda: (0, 0, 0)),
        out_shape=jax.ShapeDtypeStruct((1, rows, hw), BF16),
        compiler_params=pltpu.CompilerParams(vmem_limit_bytes=VMEM_LIMIT),
        name="v_upproj",
    )(o_lat, wv_all)


def _out_res_kernel(o_ref, w_ref, x_ref, ga_ref, y_ref):
    y_ref[0] = x_ref[0] + ga_ref[0] * _dot(o_ref[0], w_ref[...])


def _out_residual(o, w_o_pad, x, ga):
    bg, lg, d = x.shape
    tb = _tok_block(lg)
    return pl.pallas_call(
        _out_res_kernel,
        grid=(bg, lg // tb),
        in_specs=[_tok(o.shape[2], tb), _full(w_o_pad), _tok(d, tb), _modspec(ga, tb)],
        out_specs=_tok(d, tb),
        out_shape=jax.ShapeDtypeStruct((bg, lg, d), F32),
        compiler_params=_params("parallel", "parallel"),
        name="out_residual",
    )(o, w_o_pad, x, ga)


def _final_kernel(x_ref, g_ref, sh_ref, sc_ref, y_ref):
    y_ref[0] = _norm_mod(x_ref[0], g_ref[...], sh_ref[0], sc_ref[0])


def _final_norm(x, g, sh, sc):
    bg, lg, d = x.shape
    tb = _tok_block(lg)
    return pl.pallas_call(
        _final_kernel,
        grid=(bg, lg // tb),
        in_specs=[_tok(d, tb), _full(g), _modspec(sh, tb), _modspec(sc, tb)],
        out_specs=_tok(d, tb),
        out_shape=jax.ShapeDtypeStruct((bg, lg, d), F32),
        compiler_params=_params("parallel", "parallel"),
        name="final_norm",
    )(x, g, sh, sc)


def _pad_heads(w, offset=0):
    pad = [(0, 0)] * (w.ndim - 1) + [(offset, LANES - offset - w.shape[-1])]
    wp = jnp.pad(w, pad)
    return wp.reshape(wp.shape[:-2] + (wp.shape[-2] * LANES,))


def _attn_weights(w_dkv, g_ckv, w_kr, w_ukv, nope, n_heads):
    r_dim = w_kr.shape[1]
    pk = jnp.tile(jnp.pad(jnp.eye(r_dim, dtype=F32), ((0, 0), (nope, LANES - nope - r_dim))), (1, n_heads))
    return dict(w_dkv=w_dkv, g_ckv=g_ckv.reshape(1, -1), w_kr=w_kr, w_kr_rot=_rot_cols(w_kr),
                wk_all=_pad_heads(w_ukv[..., :nope]).astype(BF16),
                wv_all=_pad_heads(w_ukv[..., nope:]).astype(BF16),
                pk=pk.astype(BF16))


def _query_weights(w_dq, g_cq, w_uq, w_o, nope, rope, n_heads):
    q_lora = w_uq.shape[0]
    wq = w_uq.reshape(q_lora, n_heads, nope + rope)
    w_pe = wq[..., nope:]
    wq_a = _pad_heads(wq)
    wq_b = _pad_heads(_rot_cols(w_pe), offset=nope)
    v_head = w_o.shape[0] // n_heads
    w_o_pad = jnp.pad(w_o.reshape(n_heads, v_head, -1), ((0, 0), (0, LANES - v_head), (0, 0)))
    return dict(w_dq=w_dq.astype(BF16), g_cq=g_cq.reshape(1, -1), wq_a=wq_a.astype(BF16),
                wq_b=wq_b.astype(BF16), w_o_pad=w_o_pad.reshape(n_heads * LANES, -1).astype(BF16))


def _query_tables(positions, nope, rope, scale):
    cos, sin = _rope_tables(positions, rope, scale)
    n = positions.shape[0]
    tail = LANES - nope - rope
    cos_t = jnp.concatenate([jnp.full((n, nope), scale, F32), cos, jnp.zeros((n, tail), F32)], -1)
    sin_t = jnp.concatenate([jnp.zeros((n, nope), F32), sin, jnp.zeros((n, tail), F32)], -1)
    return cos_t, sin_t


def kernel(x_prompt, x_sample, state_ssm_re, state_ssm_im, cache_kv_latent, cache_k_rope, page_table,
           c_prompt, c_sample,
           w_mod, b_mod, g_norm,
           ssm_a_re, ssm_a_im, ssm_log_dt, ssm_b_re, ssm_b_im, ssm_c_re, ssm_c_im, ssm_d, ssm_w_glu,
           w_mod_kv, b_mod_kv, g_kv_norm, w_dkv, g_ckv, w_kr, w_ukv,
           w_dq, g_cq, w_uq, w_o,
           peer_w_q, peer_keys, peer_u, peer_v,
           w_mod_final, b_mod_final, g_final):
    bsz, seq, d = x_prompt.shape
    bd, dec_seq, _ = x_sample.shape
    assert dec_seq == 1
    depth = w_mod.shape[0]
    n_a = ssm_a_re.shape[0]
    n_groups, n_state = ssm_a_re.shape[1], ssm_a_re.shape[2]
    n_heads = w_ukv.shape[1]
    rope = w_kr.shape[1]
    nope = w_uq.shape[2] // n_heads - rope
    page = cache_kv_latent.shape[1]
    past_len = page_table.shape[1] * page
    sm_scale = 1.0 / math.sqrt(nope + rope)

    c_all = jnp.concatenate([c_prompt, c_sample], axis=0)

    def grouped(m, n):
        parts = jnp.split(m, n, axis=-1)
        return [q[:bsz][:, None, :] for q in parts], [q[bsz:][None] for q in parts]

    mods = [grouped(_mod_linear(c_all, w_mod[l], b_mod[l]), 6) for l in range(depth)]
    mod_kv = grouped(_mod_linear(c_all, w_mod_kv, b_mod_kv), 2)
    mod_f = grouped(_mod_linear(c_all, w_mod_final, b_mod_final), 2)

    s5 = [_s5_prep(ssm_a_re[l], ssm_a_im[l], ssm_log_dt[l], ssm_b_re[l], ssm_b_im[l],
                   ssm_c_re[l], ssm_c_im[l]) for l in range(n_a)]
    glu_w = [ssm_w_glu[l].astype(BF16) for l in range(n_a)]
    peer_w = []
    for l in range(depth):
        wqh, wql = _hi_lo(peer_w_q[l])
        peer_w.append(dict(wqh=wqh, wql=wql, keys=peer_keys[l], u=peer_u[l].astype(BF16),
                           vt=peer_v[l].T.astype(BF16)))
    aw = _attn_weights(w_dkv, g_ckv, w_kr, w_ukv, nope, n_heads)
    qw = [_query_weights(w_dq[j], g_cq[j], w_uq[j], w_o[j], nope, rope, n_heads) for j in range(depth - n_a)]

    pos_p = jnp.arange(seq, dtype=jnp.int32)
    pos_s = past_len + jnp.arange(dec_seq, dtype=jnp.int32)
    tables = []
    for pos, q_scale in ((pos_p, sm_scale * math.log2(math.e)), (pos_s, sm_scale)):
        tables.append(dict(k=_rope_tables(pos, rope), q=_query_tables(pos, nope, rope, q_scale)))

    def gvec(v):
        return v.reshape(1, -1)

    def run(x, grp, h0_re, h0_im):
        is_prompt = grp == 0
        tab = tables[grp]
        ssm_re, ssm_im = [], []
        ckv = kr = kcat = vpad = None
        for layer in range(depth):
            sh1, sc1, ga1, sh2, sc2, ga2 = mods[layer][grp]
            if layer == n_a:
                mk = mod_kv[grp]
                outs = _kv_proj(x, gvec(g_kv_norm), mk[0], mk[1], aw, *tab["k"], heads_out=is_prompt)
                if is_prompt:
                    ckv, kr, kcat, vpad = outs
                else:
                    ckv, kr = outs
            if layer < n_a:
                if is_prompt:
                    z, hr, hi = _s5_prompt(x, gvec(g_norm[layer, 0]), sh1, sc1, s5[layer], gvec(ssm_d[layer]))
                else:
                    z, hr, hi = _s5_step(x, gvec(g_norm[layer, 0]), sh1, sc1, h0_re[layer], h0_im[layer],
                                         s5[layer], gvec(ssm_d[layer]))
                ssm_re.append(hr.reshape(-1, n_groups, n_state))
                ssm_im.append(hi.reshape(-1, n_groups, n_state))
                x = _glu_residual(z, glu_w[layer], x, ga1)
            else:
                lw = qw[layer - n_a]
                if is_prompt:
                    q = _q_proj(x, gvec(g_norm[layer, 0]), sh1, sc1, lw, *tab["q"], n_heads, BF16)
                    o = _flash_attention(q, kcat, vpad, n_heads)
                else:
                    q = _q_proj(x, gvec(g_norm[layer, 0]), sh1, sc1, lw, *tab["q"], n_heads, F32)
                    qlat, qpe = _absorb(q[0], aw["wk_all"], n_heads, nope, rope)
                    rows = qlat.shape[0]
                    o_lat = _decode_attention(qlat.reshape(rows, n_heads, -1), qpe.reshape(rows, n_heads, -1),
                                              ckv.reshape(rows, 1, -1), kr.reshape(rows, 1, -1),
                                              cache_kv_latent, cache_k_rope.transpose(0, 2, 1), page_table)
                    o = _upproj(o_lat.reshape(rows, -1), aw["wv_all"], n_heads)
                x = _out_residual(o, lw["w_o_pad"], x, ga1)
            x = _peer(x, gvec(g_norm[layer, 1]), sh2, sc2, ga2, peer_w[layer])
        shf, scf = mod_f[grp]
        y = _final_norm(x, gvec(g_final), shf, scf)
        return y, jnp.stack(ssm_re), jnp.stack(ssm_im), ckv, kr

    gn = n_groups * n_state
    h0_re = state_ssm_re.reshape(n_a, bd, gn)
    h0_im = state_ssm_im.reshape(n_a, bd, gn)
    y_s, re_s, im_s, ckv_s, kr_s = run(x_sample.reshape(1, bd, d), 1, h0_re, h0_im)
    y_p, re_p, im_p, ckv_p, kr_p = run(x_prompt, 0, None, None)
    return (y_p, y_s.reshape(bd, dec_seq, d), re_p, im_p, ckv_p, kr_p,
            re_s, im_s, ckv_s.reshape(bd, dec_seq, -1), kr_s.reshape(bd, dec_seq, -1))
```

```python
import functools
import math

import jax
import jax.numpy as jnp
from jax import lax
from jax.experimental import pallas as pl
from jax.experimental.pallas import tpu as pltpu

F32 = jnp.float32
BF16 = jnp.bfloat16
EPS = 1e-6
ROPE_THETA = 10000.0
PEER_TOPK = 16
NEG_INF = -1e30
LANES = 128
PEER_GATE_W = 512
FLASH_HEADS_PER_STEP = 4
PEER_ROUTE_W = 256
PEER_SUB_CHUNK = 1024
PEER_SUBS_PER_STEP = 2
DECODE_PAGES_PER_CHUNK = 64
DECODE_CHAINS = 1
SSM_GROUPS_PER_BLOCK = 16
VMEM_LIMIT = 56 * 1024 * 1024

_NN = (((1,), (0,)), ((), ()))
_NT = (((1,), (1,)), ((), ()))


def _dot(a, b, dims=_NN):
    return lax.dot_general(a, b, dims, preferred_element_type=F32)


def _split(a):
    hi = a.astype(BF16)
    lo = (a - hi.astype(F32)).astype(BF16)
    return hi, lo


def _dot3(a, b, dims=_NN):
    ah, al = _split(a)
    bh, bl = _split(b)
    return _dot(ah, bh, dims) + (_dot(ah, bl, dims) + _dot(al, bh, dims))


def _dot3w(a, wh, wl, dims=_NN):
    ah, al = _split(a)
    return _dot(ah, wh, dims) + (_dot(ah, wl, dims) + _dot(al, wh, dims))


def _sigmoid(x):
    return 1.0 / (1.0 + jnp.exp(-x))


def _rms(x, g):
    return x * lax.rsqrt(jnp.mean(x * x, axis=-1, keepdims=True) + EPS) * g


def _norm_mod(x, g, sh, sc):
    return _rms(x, g) * (1.0 + sc) + sh


def _params(*sem, flags=None):
    return pltpu.CompilerParams(dimension_semantics=sem, vmem_limit_bytes=VMEM_LIMIT, flags=flags)


def _full(a):
    nd = a.ndim
    return pl.BlockSpec(a.shape, lambda *_: (0,) * nd)


def _tok(width, tb):
    return pl.BlockSpec((1, tb, width), lambda b, i, *_: (b, i, 0))


def _modspec(m, tb):
    if m.shape[1] == 1:
        return pl.BlockSpec((1, 1, m.shape[2]), lambda b, i, *_: (b, 0, 0))
    return pl.BlockSpec((1, tb, m.shape[2]), lambda b, i, *_: (b, i, 0))


def _tok_block(lg, cap=512):
    tb = min(lg, cap)
    assert lg % tb == 0
    return tb


def _hi_lo(w):
    hi = lax.bitcast_convert_type(lax.bitcast_convert_type(w, jnp.uint32) & jnp.uint32(0xFFFF0000), F32)
    return hi.astype(BF16), (w - hi).astype(BF16)


def _mod_kernel(c_ref, wh_ref, wl_ref, b_ref, o_ref):
    c = c_ref[...]
    o_ref[...] = _dot3w(c * _sigmoid(c), wh_ref[...], wl_ref[...]) + b_ref[...]


def _mod_linear(c, w, b):
    bc, d = c.shape
    n = w.shape[1]
    tn = min(n, 1024)
    assert n % tn == 0
    wh, wl = _hi_lo(w)
    return pl.pallas_call(
        _mod_kernel,
        grid=(n // tn,),
        in_specs=[pl.BlockSpec((bc, d), lambda j: (0, 0)),
                  pl.BlockSpec((d, tn), lambda j: (0, j)),
                  pl.BlockSpec((d, tn), lambda j: (0, j)),
                  pl.BlockSpec((1, tn), lambda j: (0, j))],
        out_specs=pl.BlockSpec((bc, tn), lambda j: (0, j)),
        out_shape=jax.ShapeDtypeStruct((bc, n), F32),
        compiler_params=_params("parallel"),
        name="mod_linear",
    )(c, wh, wl, b.reshape(1, n))


def _s5_prep(a_re, a_im, log_dt, b_re, b_im, c_re, c_im):
    g, n, p = b_re.shape
    gb = min(g, SSM_GROUPS_PER_BLOCK)
    assert g % gb == 0
    k = g // gb
    dt = jnp.exp(log_dt.astype(F32))[:, None]
    a_re = a_re.astype(F32)
    a_im = a_im.astype(F32)
    er = jnp.exp(a_re * dt)
    lr = er * jnp.cos(a_im * dt)
    li = er * jnp.sin(a_im * dt)
    den = a_re * a_re + a_im * a_im
    kr = ((lr - 1.0) * a_re + li * a_im) / den
    ki = (li * a_re - (lr - 1.0) * a_im) / den
    bbr = kr[..., None] * b_re - ki[..., None] * b_im
    bbi = kr[..., None] * b_im + ki[..., None] * b_re
    eye = jnp.eye(gb, dtype=F32)

    def blk_in(b):
        bt = b.reshape(k, gb, n, p).transpose(0, 1, 3, 2)
        return (bt[:, :, :, None, :] * eye[None, :, None, :, None]).reshape(k, gb * p, gb * n)

    def blk_out(c):
        ct = c.reshape(k, gb, p, n).transpose(0, 1, 3, 2)
        return (ct[:, :, :, None, :] * eye[None, :, None, :, None]).reshape(k, gb * n, gb * p)

    b_blk = jnp.concatenate([blk_in(bbr), blk_in(bbi)], axis=-1)
    c_blk = jnp.concatenate([blk_out(c_re.astype(F32)), -blk_out(c_im.astype(F32))], axis=1)
    bh, bl = _hi_lo(b_blk)
    return dict(bh=bh, bl=bl, c=c_blk.astype(BF16), lam_re=lr.reshape(1, g * n), lam_im=li.reshape(1, g * n))


def _s5_input(h, bh_ref, bl_ref, k, w):
    hs = h[:, k * w:(k + 1) * w]
    return _dot3w(hs, bh_ref[k], bl_ref[k])


def _s5_prompt_kernel(x_ref, g_ref, sh_ref, sc_ref, bh_ref, bl_ref, c_ref, lre_ref, lim_ref, d_ref,
                      z_ref, sre_ref, sim_ref, s_re, s_im, st_re, st_im, h_ref, hh_ref, hl_ref,
                      *, nblk, tc, pitch, kper):
    j = pl.program_id(0)

    @pl.when(j == 0)
    def _():
        st_re[...] = jnp.zeros_like(st_re)
        st_im[...] = jnp.zeros_like(st_im)

    bsz, _, d = x_ref.shape
    w = d // nblk
    nw = lre_ref.shape[2]
    n_slabs = nw // LANES
    rows = kper * bsz
    for b in range(bsz):
        h = _norm_mod(x_ref[b], g_ref[...], sh_ref[b], sc_ref[b])
        h_ref[b] = h
        hh, hl = _split(h)
        hh_ref[b * tc:(b + 1) * tc, :] = hh
        hl_ref[b * tc:(b + 1) * tc, :] = hl

    for pair in range(nblk // kper):
        for kk in range(kper):
            k = pair * kper + kk
            hh = hh_ref[:, k * w:(k + 1) * w]
            hl = hl_ref[:, k * w:(k + 1) * w]
            bu = _dot(hh, bh_ref[k]) + (_dot(hh, bl_ref[k]) + _dot(hl, bh_ref[k]))
            for b in range(bsz):
                r0 = (kk * bsz + b) * pitch
                for sl in range(n_slabs):
                    s_re[sl, r0:r0 + tc, :] = bu[b * tc:(b + 1) * tc, sl * LANES:(sl + 1) * LANES]
                    s_im[sl, r0:r0 + tc, :] = bu[b * tc:(b + 1) * tc, nw + sl * LANES:nw + (sl + 1) * LANES]
        lr = lre_ref[pair]
        li = lim_ref[pair]

        def step(t, carry):
            sr, si = carry
            idx = pl.ds(t, rows, stride=pitch)
            br = jnp.concatenate([s_re[sl, idx, :] for sl in range(n_slabs)], axis=1)
            bi = jnp.concatenate([s_im[sl, idx, :] for sl in range(n_slabs)], axis=1)
            nr = lr * sr - li * si + br
            ni = lr * si + li * sr + bi
            for sl in range(n_slabs):
                s_re[sl, idx, :] = nr[:, sl * LANES:(sl + 1) * LANES]
                s_im[sl, idx, :] = ni[:, sl * LANES:(sl + 1) * LANES]
            return nr, ni

        sr, si = lax.fori_loop(0, tc, step, (st_re[pair], st_im[pair]))
        st_re[pair] = sr
        st_im[pair] = si
        for kk in range(kper):
            k = pair * kper + kk
            for b in range(bsz):
                r0 = (kk * bsz + b) * pitch
                hre = jnp.concatenate([s_re[sl, r0:r0 + tc, :] for sl in range(n_slabs)], axis=1)
                him = jnp.concatenate([s_im[sl, r0:r0 + tc, :] for sl in range(n_slabs)], axis=1)
                y = _dot(hre.astype(BF16), c_ref[k, :nw, :]) + _dot(him.astype(BF16), c_ref[k, nw:, :])
                hk = h_ref[b, :, k * w:(k + 1) * w]
                z_ref[b, :, k * w:(k + 1) * w] = jax.nn.gelu(y + d_ref[:, k * w:(k + 1) * w] * hk)
    sre_ref[...] = st_re[...]
    sim_ref[...] = st_im[...]


def _s5_prompt(x, g, sh, sc, prm, d_skip):
    bsz, lg, d = x.shape
    nblk = prm["bh"].shape[0]
    gn = prm["lam_re"].shape[1]
    nw = gn // nblk
    sublanes = 8
    assert sublanes % bsz == 0 and nblk % (sublanes // bsz) == 0 and nw % LANES == 0
    kper = sublanes // bsz
    npairs = nblk // kper
    tc = _tok_block(lg, 128)
    pitch = tc + 8 if (tc // 8) % 2 == 0 else tc
    rows = kper * bsz

    def pack_lam(v):
        return jnp.repeat(v.reshape(npairs, kper, nw), bsz, axis=1)

    lre, lim = pack_lam(prm["lam_re"]), pack_lam(prm["lam_im"])
    kern = functools.partial(_s5_prompt_kernel, nblk=nblk, tc=tc, pitch=pitch, kper=kper)
    blk = pl.BlockSpec((bsz, tc, d), lambda j: (0, j, 0))
    st_spec = pl.BlockSpec((npairs, rows, nw), lambda j: (0, 0, 0))
    st_shape = jax.ShapeDtypeStruct((npairs, rows, nw), F32)
    z, sre, sim = pl.pallas_call(
        kern,
        grid=(lg // tc,),
        in_specs=[blk, _full(g), _full(sh), _full(sc),
                  _full(prm["bh"]), _full(prm["bl"]), _full(prm["c"]), _full(lre), _full(lim), _full(d_skip)],
        out_specs=[blk, st_spec, st_spec],
        out_shape=[jax.ShapeDtypeStruct((bsz, lg, d), F32), st_shape, st_shape],
        scratch_shapes=[pltpu.VMEM((nw // LANES, rows * pitch, LANES), F32),
                        pltpu.VMEM((nw // LANES, rows * pitch, LANES), F32),
                        pltpu.VMEM((npairs, rows, nw), F32), pltpu.VMEM((npairs, rows, nw), F32),
                        pltpu.VMEM((bsz, tc, d), F32), pltpu.VMEM((bsz * tc, d), BF16),
                        pltpu.VMEM((bsz * tc, d), BF16)],
        compiler_params=_params("arbitrary"),
        name="s5_prompt",
    )(x, g, sh, sc, prm["bh"], prm["bl"], prm["c"], lre, lim, d_skip)

    def unpack(s):
        return s.reshape(npairs, kper, bsz, nw).transpose(2, 0, 1, 3).reshape(bsz, gn)

    return z, unpack(sre), unpack(sim)


def _s5_step_kernel(x_ref, g_ref, sh_ref, sc_ref, h0re_ref, h0im_ref, bh_ref, bl_ref, c_ref,
                    lre_ref, lim_ref, d_ref, z_ref, sre_ref, sim_ref, *, nblk):
    h = _norm_mod(x_ref[0], g_ref[...], sh_ref[0], sc_ref[0])
    w = h.shape[1] // nblk
    nw = lre_ref.shape[1] // nblk
    for k in range(nblk):
        cols = slice(k * nw, (k + 1) * nw)
        bu = _s5_input(h, bh_ref, bl_ref, k, w)
        lr = lre_ref[:, cols]
        li = lim_ref[:, cols]
        pr = h0re_ref[:, cols]
        pi = h0im_ref[:, cols]
        nr = lr * pr - li * pi + bu[:, :nw]
        ni = lr * pi + li * pr + bu[:, nw:]
        sre_ref[:, cols] = nr
        sim_ref[:, cols] = ni
        y = _dot(nr.astype(BF16), c_ref[k, :nw, :]) + _dot(ni.astype(BF16), c_ref[k, nw:, :])
        hk = h[:, k * w:(k + 1) * w]
        z_ref[0, :, k * w:(k + 1) * w] = jax.nn.gelu(y + d_ref[:, k * w:(k + 1) * w] * hk)


def _s5_step(x, g, sh, sc, h0_re, h0_im, prm, d_skip):
    _, rows, d = x.shape
    nblk = prm["bh"].shape[0]
    gn = prm["lam_re"].shape[1]
    tb = _tok_block(rows, 128)
    st_spec = pl.BlockSpec((tb, gn), lambda b, i: (i, 0))
    return pl.pallas_call(
        functools.partial(_s5_step_kernel, nblk=nblk),
        grid=(1, rows // tb),
        in_specs=[_tok(d, tb), _full(g), _modspec(sh, tb), _modspec(sc, tb), st_spec, st_spec,
                  _full(prm["bh"]), _full(prm["bl"]), _full(prm["c"]),
                  _full(prm["lam_re"]), _full(prm["lam_im"]), _full(d_skip)],
        out_specs=[_tok(d, tb), st_spec, st_spec],
        out_shape=[jax.ShapeDtypeStruct((1, rows, d), F32),
                   jax.ShapeDtypeStruct((rows, gn), F32),
                   jax.ShapeDtypeStruct((rows, gn), F32)],
        compiler_params=_params("parallel", "parallel"),
        name="s5_step",
    )(x, g, sh, sc, h0_re, h0_im, prm["bh"], prm["bl"], prm["c"], prm["lam_re"], prm["lam_im"], d_skip)


def _glu_res_kernel(z_ref, w_ref, x_ref, ga_ref, o_ref):
    y = _dot(z_ref[0].astype(BF16), w_ref[...])
    d = x_ref.shape[2]
    o_ref[0] = x_ref[0] + ga_ref[0] * (y[:, :d] * _sigmoid(y[:, d:]))


def _glu_residual(z, w_glu_bf, x, ga):
    bg, lg, d = x.shape
    tb = _tok_block(lg)
    return pl.pallas_call(
        _glu_res_kernel,
        grid=(bg, lg // tb),
        in_specs=[_tok(d, tb), _full(w_glu_bf), _tok(d, tb), _modspec(ga, tb)],
        out_specs=_tok(d, tb),
        out_shape=jax.ShapeDtypeStruct((bg, lg, d), F32),
        compiler_params=_params("parallel", "parallel"),
        name="glu_residual",
    )(z, w_glu_bf, x, ga)


def _cand_pairs(topk):
    return [(a, b) for a in range(topk + 1) for b in range(topk + 1) if (a + 1) * (b + 1) <= topk + 1]


def _top_rows(s, count, rank_count=0):
    out = []
    rank = jnp.zeros(s.shape, F32) if rank_count else None
    for r in range(count):
        if r == 0:
            cur = s
        else:
            below = s < out[-1]
            cur = jnp.where(below, s, NEG_INF)
            if r <= rank_count:
                rank = jnp.where(below, float(r), rank)
        out.append(jnp.max(cur, axis=0, keepdims=True))
    return out, rank


def _peer_route_kernel(x_ref, g_ref, sh_ref, sc_ref, wqh_ref, wql_ref, keys_ref,
                       hb_ref, r1_ref, e1_ref, nb_ref, coef_ref, cand_ref, *, topk):
    h = _norm_mod(x_ref[0], g_ref[...], sh_ref[0], sc_ref[0])
    hb_ref[0] = h.astype(BF16)
    q = _dot3w(h, wqh_ref[...], wql_ref[...])
    n_heads, _, n_keys, dk = keys_ref.shape
    pairs = _cand_pairs(topk)
    cand_ref[...] = jnp.full(cand_ref.shape, NEG_INF, F32)
    tb = q.shape[0]
    gw = r1_ref.shape[4]
    rw = cand_ref.shape[1]
    for hd, t0 in [(hd, t0) for hd in range(n_heads) for t0 in range(0, tb, rw)]:
        st, tops, rank1 = [], [], None
        for s in range(2):
            qs = q[t0:t0 + rw, (hd * 2 + s) * dk:(hd * 2 + s + 1) * dk]
            sc = _dot3(keys_ref[hd, s], qs, _NT)
            st.append(sc)
            top, rank = _top_rows(sc, topk + 1, rank_count=topk * s)
            tops.append(top)
            rank1 = rank
        for r, (a, b) in enumerate(pairs):
            cand_ref[pl.ds(r, 1), :] = tops[0][a] + tops[1][b]
        cand = cand_ref[...]
        best, _ = _top_rows(cand, topk + 1)
        tau = 0.5 * (best[topk - 1] + best[topk])
        mtot = tops[0][0] + tops[1][0]
        z = jnp.sum(jnp.where(cand >= tau, jnp.exp(cand - mtot), 0.0), axis=0, keepdims=True)
        nb = jnp.zeros(st[0].shape, F32)
        for b in range(topk):
            nb = jnp.where(st[0] >= tau - tops[1][b], float(b + 1), nb)
        outs = ((r1_ref, rank1.astype(BF16)), (e1_ref, jnp.exp(st[1] - tops[1][0]).astype(BF16)),
                (nb_ref, nb), (coef_ref, jnp.exp(st[0] - tops[0][0]) / z))
        for ref, val in outs:
            val = pltpu.bitcast(val, jnp.uint32) if val.dtype == BF16 else val
            if rw >= gw:
                for k in range(rw // gw):
                    ref[0, hd, t0 // gw + k] = val[:, k * gw:(k + 1) * gw]
            else:
                ref[0, hd, t0 // gw, :, t0 % gw:t0 % gw + rw] = val


def _peer_route(x, g, sh, sc, wqh, wql, keys, topk=PEER_TOPK):
    bg, lg, d = x.shape
    n_heads, _, n_keys, _ = keys.shape
    tb = _tok_block(lg)
    n_cand = -(-len(_cand_pairs(topk)) // 8) * 8
    gw = min(tb, PEER_GATE_W)

    def gate(rows):
        return pl.BlockSpec((1, n_heads, tb // gw, rows, gw), lambda b, i: (b, 0, i, 0, 0))

    def gate_shape(rows, dt):
        return jax.ShapeDtypeStruct((bg, n_heads, lg // gw, rows, gw), dt)

    packed = n_keys // 2
    gate_shapes = [gate_shape(packed, jnp.uint32), gate_shape(packed, jnp.uint32),
                   gate_shape(n_keys, F32), gate_shape(n_keys, F32)]
    return pl.pallas_call(
        functools.partial(_peer_route_kernel, topk=topk),
        grid=(bg, lg // tb),
        in_specs=[_tok(d, tb), _full(g), _modspec(sh, tb), _modspec(sc, tb),
                  _full(wqh), _full(wql), _full(keys)],
        out_specs=[_tok(d, tb), gate(packed), gate(packed), gate(n_keys), gate(n_keys)],
        out_shape=[jax.ShapeDtypeStruct((bg, lg, d), BF16)] + gate_shapes,
        scratch_shapes=[pltpu.VMEM((n_cand, min(tb, PEER_ROUTE_W)), F32)],
        compiler_params=_params("parallel", "parallel"),
        name="peer_route",
    )(x, g, sh, sc, wqh, wql, keys)


def _peer_dense_kernel(hb_ref, r1_ref, e1_ref, nb_ref, coef_ref, u_ref, vt_ref, x_ref, ga_ref,
                       o_ref, acc_ref, *gate_refs, n_keys):
    c = pl.program_id(2)
    n_steps = pl.num_programs(2)
    n_heads = r1_ref.shape[1]
    gw = r1_ref.shape[4]
    tb = hb_ref.shape[1]
    n_sub = len(gate_refs)
    ec = u_ref.shape[0] // n_sub
    per = ec // n_keys

    def build_gate(i0, n_i, dst):
        for ii in range(n_i):
            for lt in range(tb // gw):
                gate = None
                for hd in range(n_heads):
                    nb = nb_ref[0, hd, lt, pl.ds(i0 + ii, 1), :].astype(BF16)
                    coef = coef_ref[0, hd, lt, pl.ds(i0 + ii, 1), :].astype(BF16)
                    r1 = pltpu.bitcast(r1_ref[0, hd, lt], BF16)
                    e1 = pltpu.bitcast(e1_ref[0, hd, lt], BF16)
                    term = jnp.where(r1 < nb, e1 * coef, jnp.zeros((), BF16))
                    gate = term if gate is None else gate + term
                dst[ii * n_keys:(ii + 1) * n_keys, lt * gw:(lt + 1) * gw] = gate

    @pl.when(c == 0)
    def _():
        acc_ref[...] = jnp.zeros_like(acc_ref)

    acts = []
    for s in range(n_sub):
        rows = slice(s * ec, (s + 1) * ec)
        acts.append(jax.nn.gelu(_dot(u_ref[rows, :], hb_ref[0], _NT).astype(BF16)))
    for s in range(n_sub):
        rows = slice(s * ec, (s + 1) * ec)
        build_gate((c * n_sub + s) * per, per, gate_refs[s])
        acc_ref[...] += _dot(vt_ref[:, rows], acts[s] * gate_refs[s][...])

    @pl.when(c == n_steps - 1)
    def _():
        o_ref[0] = x_ref[0] + ga_ref[0] * acc_ref[...].T


def _peer_dense(hb, gates, u_bf, vt_bf, x, ga):
    bg, lg, d = x.shape
    n_heads, n_keys = gates[2].shape[1], gates[2].shape[3]
    n_exp = u_bf.shape[0]
    tb = _tok_block(lg)
    sub = min(n_exp, PEER_SUB_CHUNK)
    n_sub = math.gcd(n_exp // sub, PEER_SUBS_PER_STEP)
    ec = n_sub * sub
    assert n_exp % ec == 0 and sub % n_keys == 0

    def gate(a):
        gw = a.shape[4]
        return pl.BlockSpec((1, n_heads, tb // gw, a.shape[3], gw), lambda b, i, c: (b, 0, i, 0, 0))

    return pl.pallas_call(
        functools.partial(_peer_dense_kernel, n_keys=n_keys),
        grid=(bg, lg // tb, n_exp // ec),
        in_specs=[_tok(d, tb)] + [gate(a) for a in gates] + [
                  pl.BlockSpec((ec, d), lambda b, i, c: (c, 0)),
                  pl.BlockSpec((d, ec), lambda b, i, c: (0, c)),
                  _tok(d, tb), _modspec(ga, tb)],
        out_specs=_tok(d, tb),
        out_shape=jax.ShapeDtypeStruct((bg, lg, d), F32),
        scratch_shapes=[pltpu.VMEM((d, tb), F32)] + [pltpu.VMEM((sub, tb), BF16)] * n_sub,
        compiler_params=_params("parallel", "parallel", "arbitrary"),
        name="peer_dense",
    )(hb, *gates, u_bf, vt_bf, x, ga)


def _peer(x, g, sh, sc, ga, pw):
    hb, *gates = _peer_route(x, g, sh, sc, pw["wqh"], pw["wql"], pw["keys"])
    return _peer_dense(hb, gates, pw["u"], pw["vt"], x, ga)


def _rope_tables(positions, rope_dim, scale=1.0):
    inv = ROPE_THETA ** (-jnp.arange(0, rope_dim, 2, dtype=F32) / rope_dim)
    ang = positions.astype(F32)[:, None] * inv
    cos, sin = jnp.cos(ang), jnp.sin(ang)
    return jnp.concatenate([cos, cos], -1) * scale, jnp.concatenate([sin, sin], -1) * scale


def _rot_cols(w):
    half = w.shape[-1] // 2
    return jnp.concatenate([-w[..., half:], w[..., :half]], axis=-1)


def _kv_kernel(x_ref, g_ref, sh_ref, sc_ref, wdkv_ref, gckv_ref, wkr_ref, wkrr_ref, cos_ref, sin_ref,
               *rest, heads_out):
    h = _norm_mod(x_ref[0], g_ref[...], sh_ref[0], sc_ref[0])
    ckv = _rms(_dot3(h, wdkv_ref[...]), gckv_ref[...])
    kr = _dot3(h, wkr_ref[...]) * cos_ref[...] + _dot3(h, wkrr_ref[...]) * sin_ref[...]
    if heads_out:
        wk_ref, pk_ref, wv_ref, ckv_ref, kr_ref, kcat_ref, v_ref = rest
        cb = ckv.astype(BF16)
        kcat_ref[0] = (_dot(cb, wk_ref[...]) + _dot(kr.astype(BF16), pk_ref[...])).astype(BF16)
        v_ref[0] = _dot(cb, wv_ref[...]).astype(BF16)
    else:
        ckv_ref, kr_ref = rest
    ckv_ref[0] = ckv
    kr_ref[0] = kr


def _kv_proj(x, g, sh, sc, aw, cos, sin, heads_out):
    bg, lg, d = x.shape
    tb = _tok_block(lg)
    c_dim, r_dim = aw["w_dkv"].shape[1], aw["w_kr"].shape[1]
    tab = _full(cos) if cos.shape[0] == 1 else pl.BlockSpec((tb, r_dim), lambda b, i: (i, 0))
    ins = [x, g, sh, sc, aw["w_dkv"], aw["g_ckv"], aw["w_kr"], aw["w_kr_rot"], cos, sin]
    in_specs = [_tok(d, tb), _full(g), _modspec(sh, tb), _modspec(sc, tb), _full(aw["w_dkv"]),
                _full(aw["g_ckv"]), _full(aw["w_kr"]), _full(aw["w_kr_rot"]), tab, tab]
    out_specs = [_tok(c_dim, tb), _tok(r_dim, tb)]
    out_shape = [jax.ShapeDtypeStruct((bg, lg, c_dim), F32), jax.ShapeDtypeStruct((bg, lg, r_dim), F32)]
    if heads_out:
        hw = aw["wk_all"].shape[1]
        ins += [aw["wk_all"], aw["pk"], aw["wv_all"]]
        in_specs += [_full(aw["wk_all"]), _full(aw["pk"]), _full(aw["wv_all"])]
        out_specs += [_tok(hw, tb), _tok(hw, tb)]
        out_shape += [jax.ShapeDtypeStruct((bg, lg, hw), BF16)] * 2
    return pl.pallas_call(
        functools.partial(_kv_kernel, heads_out=heads_out),
        grid=(bg, lg // tb),
        in_specs=in_specs, out_specs=out_specs, out_shape=out_shape,
        compiler_params=_params("parallel", "parallel"),
        name="kv_proj",
    )(*ins)


def _q_kernel(x_ref, g_ref, sh_ref, sc_ref, wdq_ref, gcq_ref, wqa_ref, wqb_ref, cos_ref, sin_ref, q_ref,
              *, n_heads):
    h = _norm_mod(x_ref[0], g_ref[...], sh_ref[0], sc_ref[0])
    cq = _rms(_dot(h.astype(BF16), wdq_ref[...]), gcq_ref[...]).astype(BF16)
    qa = _dot(cq, wqa_ref[...])
    qb = _dot(cq, wqb_ref[...])
    cos = cos_ref[...]
    sin = sin_ref[...]
    for hd in range(n_heads):
        cols = slice(hd * LANES, (hd + 1) * LANES)
        q_ref[0, :, cols] = (qa[:, cols] * cos + qb[:, cols] * sin).astype(q_ref.dtype)


def _q_proj(x, g, sh, sc, lw, cos, sin, n_heads, out_dtype):
    bg, lg, d = x.shape
    tb = _tok_block(lg)
    hw = n_heads * LANES
    tab = _full(cos) if cos.shape[0] == 1 else pl.BlockSpec((tb, LANES), lambda b, i: (i, 0))
    return pl.pallas_call(
        functools.partial(_q_kernel, n_heads=n_heads),
        grid=(bg, lg // tb),
        in_specs=[_tok(d, tb), _full(g), _modspec(sh, tb), _modspec(sc, tb), _full(lw["w_dq"]),
                  _full(lw["g_cq"]), _full(lw["wq_a"]), _full(lw["wq_b"]), tab, tab],
        out_specs=_tok(hw, tb),
        out_shape=jax.ShapeDtypeStruct((bg, lg, hw), out_dtype),
        compiler_params=_params("parallel", "parallel"),
        name="q_proj",
    )(x, g, sh, sc, lw["w_dq"], lw["g_cq"], lw["wq_a"], lw["wq_b"], cos, sin)


def _flash_kernel(q_ref, k_ref, v_ref, o_ref, m_ref, l_ref, acc_ref, *, tq, hps):
    qi = pl.program_id(2)
    m_ref[...] = jnp.full(m_ref.shape, NEG_INF, F32)
    l_ref[...] = jnp.zeros_like(l_ref)
    acc_ref[...] = jnp.zeros_like(acc_ref)

    def update(j, masked):
        start = pl.multiple_of(j * tq, tq)
        for hd in range(hps):
            cols = slice(hd * LANES, (hd + 1) * LANES)
            k = k_ref[0, pl.ds(start, tq), cols]
            v = v_ref[0, pl.ds(start, tq), cols]
            s = _dot(q_ref[0, :, cols], k, _NT)
            if masked:
                row = lax.broadcasted_iota(jnp.int32, s.shape, 0)
                col = lax.broadcasted_iota(jnp.int32, s.shape, 1)
                s = jnp.where(col <= row, s, NEG_INF)
            m_old = m_ref[hd]
            m_new = jnp.maximum(m_old, jnp.max(s, axis=-1, keepdims=True))
            corr = jnp.exp2(m_old - m_new)
            p = jnp.exp2(s - jnp.concatenate([m_new] * (tq // LANES), axis=1))
            l_ref[hd] = l_ref[hd] * corr + jnp.sum(p, axis=-1, keepdims=True)
            acc_ref[hd] = acc_ref[hd] * corr + _dot(p.astype(BF16), v)
            m_ref[hd] = m_new

    def body(j, carry):
        update(j, False)
        return carry

    lax.fori_loop(0, qi, body, 0)
    update(qi, True)
    for hd in range(hps):
        o_ref[0, :, hd * LANES:(hd + 1) * LANES] = (acc_ref[hd] / l_ref[hd]).astype(o_ref.dtype)


def _flash_attention(q, k, v, n_heads):
    bsz, lg, _ = q.shape
    tq = _tok_block(lg)
    hps = math.gcd(n_heads, FLASH_HEADS_PER_STEP)
    seq = pl.BlockSpec((1, lg, hps * LANES), lambda b, h, i: (b, 0, h))
    blk = pl.BlockSpec((1, tq, hps * LANES), lambda b, h, i: (b, i, h))
    stat = pltpu.VMEM((hps, tq, LANES), F32)
    return pl.pallas_call(
        functools.partial(_flash_kernel, tq=tq, hps=hps),
        grid=(bsz, n_heads // hps, lg // tq),
        in_specs=[blk, seq, seq],
        out_specs=blk,
        out_shape=jax.ShapeDtypeStruct(q.shape, BF16),
        scratch_shapes=[stat, stat, stat],
        compiler_params=_params("parallel", "parallel", "arbitrary"),
        name="flash_attention",
    )(q, k, v)


def _absorb_kernel(q_ref, wk_ref, qlat_ref, qpe_ref, *, n_heads, nope, rope):
    c_dim = wk_ref.shape[0]
    for hd in range(n_heads):
        qh = q_ref[:, hd * LANES:(hd + 1) * LANES]
        wk = wk_ref[:, hd * LANES:(hd + 1) * LANES]
        qlat_ref[:, hd * c_dim:(hd + 1) * c_dim] = _dot(qh.astype(BF16), wk, _NT)
        qpe_ref[:, hd * rope:(hd + 1) * rope] = qh[:, nope:nope + rope]


def _absorb(q, wk_all, n_heads, nope, rope):
    rows = q.shape[0]
    c_dim = wk_all.shape[0]
    return pl.pallas_call(
        functools.partial(_absorb_kernel, n_heads=n_heads, nope=nope, rope=rope),
        in_specs=[_full(q), _full(wk_all)],
        out_specs=[pl.BlockSpec((rows, n_heads * c_dim), lambda: (0, 0)),
                   pl.BlockSpec((rows, n_heads * rope), lambda: (0, 0))],
        out_shape=[jax.ShapeDtypeStruct((rows, n_heads * c_dim), F32),
                   jax.ShapeDtypeStruct((rows, n_heads * rope), F32)],
        compiler_params=pltpu.CompilerParams(vmem_limit_bytes=VMEM_LIMIT),
        name="q_absorb",
    )(q, wk_all)


def _decode_kernel(pt_ref, qlat_ref, qpe_ref, cnew_ref, rnew_ref, cc_hbm, cr_hbm, o_ref,
                   kc_buf, kr_buf, sem_c, sem_r, *, pages, n_chunks, chains):
    b = pl.program_id(0)
    n_rows = pl.num_programs(0)

    def chunk_copies(row, chunk, slot, src_page=None):
        cps = []
        for j in range(pages):
            pg = pt_ref[row, chunk * pages + j] if src_page is None else src_page
            cps.append(pltpu.make_async_copy(cc_hbm.at[pg], kc_buf.at[slot, j], sem_c.at[slot]))
            cps.append(pltpu.make_async_copy(cr_hbm.at[pg], kr_buf.at[slot, j], sem_r.at[slot]))
        return cps

    def start_chunk(row, chunk, slot):
        for n, cp in enumerate(chunk_copies(row, chunk, slot)):
            cp.start(priority=(n // 2) % 2)

    def wait_chunk(slot):
        for cp in chunk_copies(0, 0, slot, src_page=0):
            cp.wait()

    @pl.when(b == 0)
    def _():
        start_chunk(0, 0, 0)

    ql = qlat_ref[0]
    qp = qpe_ref[0]
    qlb = ql.astype(BF16)
    qpb = qp.astype(BF16)
    per = pages // chains
    n_heads, c_dim = ql.shape
    state = [(jnp.full((n_heads, 1), NEG_INF, F32), jnp.zeros((n_heads, 1), F32),
              jnp.zeros((n_heads, c_dim), F32)) for _ in range(chains)]
    for k in range(n_chunks):
        slot = k % 2
        if k + 1 < n_chunks:
            start_chunk(b, k + 1, 1 - slot)
        else:
            @pl.when(b + 1 < n_rows)
            def _():
                start_chunk(b + 1, 0, 1 - slot)
        wait_chunk(slot)
        for ch in range(chains):
            js = range(ch * per, (ch + 1) * per)
            kc = jnp.concatenate([kc_buf[slot, j].astype(BF16) for j in js], axis=0)
            kr = jnp.concatenate([kr_buf[slot, j].astype(BF16) for j in js], axis=1)
            s = _dot(qlb, kc, _NT) + _dot(qpb, kr)
            m_old, l_old, acc = state[ch]
            m_new = jnp.maximum(m_old, jnp.max(s, axis=-1, keepdims=True))
            corr = jnp.exp(m_old - m_new)
            p = jnp.exp(s - m_new)
            state[ch] = (m_new, l_old * corr + jnp.sum(p, axis=-1, keepdims=True),
                         acc * corr + _dot(p.astype(BF16), kc))

    cn = cnew_ref[0]
    rn = rnew_ref[0]
    s_new = jnp.sum(ql * cn, axis=-1, keepdims=True) + jnp.sum(qp * rn, axis=-1, keepdims=True)
    m_fin = s_new
    for m, _, _ in state:
        m_fin = jnp.maximum(m_fin, m)
    p_new = jnp.exp(s_new - m_fin)
    l_fin = p_new
    acc_fin = p_new * cn
    for m, l, acc in state:
        w = jnp.exp(m - m_fin)
        l_fin = l_fin + l * w
        acc_fin = acc_fin + acc * w
    o_ref[0] = acc_fin / l_fin


def _decode_attention(qlat, qpe, ckv_new, kr_new, cache_c, cache_rt, page_table):
    bd, n_heads, c_dim = qlat.shape
    r_dim = qpe.shape[2]
    page = cache_c.shape[1]
    n_pages = page_table.shape[1]
    pages = max(p for p in range(1, DECODE_PAGES_PER_CHUNK + 1)
                if n_pages % p == 0 and (n_pages // p) % 2 == 0)
    chains = math.gcd(pages, DECODE_CHAINS)
    per_b3 = lambda w: pl.BlockSpec((1, n_heads, w), lambda b, pt: (b, 0, 0))
    new3 = lambda w: pl.BlockSpec((1, 1, w), lambda b, pt: (b, 0, 0))
    grid_spec = pltpu.PrefetchScalarGridSpec(
        num_scalar_prefetch=1,
        grid=(bd,),
        in_specs=[per_b3(c_dim), per_b3(r_dim), new3(c_dim), new3(r_dim),
                  pl.BlockSpec(memory_space=pl.ANY), pl.BlockSpec(memory_space=pl.ANY)],
        out_specs=per_b3(c_dim),
        scratch_shapes=[pltpu.VMEM((2, pages, page, c_dim), F32), pltpu.VMEM((2, pages, r_dim, page), F32),
                        pltpu.SemaphoreType.DMA((2,)), pltpu.SemaphoreType.DMA((2,))],
    )
    return pl.pallas_call(
        functools.partial(_decode_kernel, pages=pages, n_chunks=n_pages // pages, chains=chains),
        grid_spec=grid_spec,
        out_shape=jax.ShapeDtypeStruct((bd, n_heads, c_dim), F32),
        compiler_params=_params("arbitrary"),
        name="decode_attention",
    )(page_table, qlat, qpe, ckv_new, kr_new, cache_c, cache_rt)


def _upproj_kernel(ol_ref, wv_ref, o_ref, *, n_heads):
    c_dim = wv_ref.shape[0]
    for hd in range(n_heads):
        ol = ol_ref[:, hd * c_dim:(hd + 1) * c_dim].astype(BF16)
        o_ref[0, :, hd * LANES:(hd + 1) * LANES] = _dot(ol, wv_ref[:, hd * LANES:(hd + 1) * LANES]).astype(BF16)


def _upproj(o_lat, wv_all, n_heads):
    rows = o_lat.shape[0]
    hw = wv_all.shape[1]
    return pl.pallas_call(
        functools.partial(_upproj_kernel, n_heads=n_heads),
        in_specs=[_full(o_lat), _full(wv_all)],
        out_specs=pl.BlockSpec((1, rows, hw), lambda: (0, 0, 0)),
        out_shape=jax.ShapeDtypeStruct((1, rows, hw), BF16),
        compiler_params=pltpu.CompilerParams(vmem_limit_bytes=VMEM_LIMIT),
        name="v_upproj",
    )(o_lat, wv_all)


def _out_res_kernel(o_ref, w_ref, x_ref, ga_ref, y_ref):
    y_ref[0] = x_ref[0] + ga_ref[0] * _dot(o_ref[0], w_ref[...])


def _out_residual(o, w_o_pad, x, ga):
    bg, lg, d = x.shape
    tb = _tok_block(lg)
    return pl.pallas_call(
        _out_res_kernel,
        grid=(bg, lg // tb),
        in_specs=[_tok(o.shape[2], tb), _full(w_o_pad), _tok(d, tb), _modspec(ga, tb)],
        out_specs=_tok(d, tb),
        out_shape=jax.ShapeDtypeStruct((bg, lg, d), F32),
        compiler_params=_params("parallel", "parallel"),
        name="out_residual",
    )(o, w_o_pad, x, ga)


def _final_kernel(x_ref, g_ref, sh_ref, sc_ref, y_ref):
    y_ref[0] = _norm_mod(x_ref[0], g_ref[...], sh_ref[0], sc_ref[0])


def _final_norm(x, g, sh, sc):
    bg, lg, d = x.shape
    tb = _tok_block(lg)
    return pl.pallas_call(
        _final_kernel,
        grid=(bg, lg // tb),
        in_specs=[_tok(d, tb), _full(g), _modspec(sh, tb), _modspec(sc, tb)],
        out_specs=_tok(d, tb),
        out_shape=jax.ShapeDtypeStruct((bg, lg, d), F32),
        compiler_params=_params("parallel", "parallel"),
        name="final_norm",
    )(x, g, sh, sc)


def _pad_heads(w, offset=0):
    pad = [(0, 0)] * (w.ndim - 1) + [(offset, LANES - offset - w.shape[-1])]
    wp = jnp.pad(w, pad)
    return wp.reshape(wp.shape[:-2] + (wp.shape[-2] * LANES,))


def _attn_weights(w_dkv, g_ckv, w_kr, w_ukv, nope, n_heads):
    r_dim = w_kr.shape[1]
    pk = jnp.tile(jnp.pad(jnp.eye(r_dim, dtype=F32), ((0, 0), (nope, LANES - nope - r_dim))), (1, n_heads))
    return dict(w_dkv=w_dkv, g_ckv=g_ckv.reshape(1, -1), w_kr=w_kr, w_kr_rot=_rot_cols(w_kr),
                wk_all=_pad_heads(w_ukv[..., :nope]).astype(BF16),
                wv_all=_pad_heads(w_ukv[..., nope:]).astype(BF16),
                pk=pk.astype(BF16))


def _query_weights(w_dq, g_cq, w_uq, w_o, nope, rope, n_heads):
    q_lora = w_uq.shape[0]
    wq = w_uq.reshape(q_lora, n_heads, nope + rope)
    w_pe = wq[..., nope:]
    wq_a = _pad_heads(wq)
    wq_b = _pad_heads(_rot_cols(w_pe), offset=nope)
    v_head = w_o.shape[0] // n_heads
    w_o_pad = jnp.pad(w_o.reshape(n_heads, v_head, -1), ((0, 0), (0, LANES - v_head), (0, 0)))
    return dict(w_dq=w_dq.astype(BF16), g_cq=g_cq.reshape(1, -1), wq_a=wq_a.astype(BF16),
                wq_b=wq_b.astype(BF16), w_o_pad=w_o_pad.reshape(n_heads * LANES, -1).astype(BF16))


def _query_tables(positions, nope, rope, scale):
    cos, sin = _rope_tables(positions, rope, scale)
    n = positions.shape[0]
    tail = LANES - nope - rope
    cos_t = jnp.concatenate([jnp.full((n, nope), scale, F32), cos, jnp.zeros((n, tail), F32)], -1)
    sin_t = jnp.concatenate([jnp.zeros((n, nope), F32), sin, jnp.zeros((n, tail), F32)], -1)
    return cos_t, sin_t


def kernel(x_prompt, x_sample, state_ssm_re, state_ssm_im, cache_kv_latent, cache_k_rope, page_table,
           c_prompt, c_sample,
           w_mod, b_mod, g_norm,
           ssm_a_re, ssm_a_im, ssm_log_dt, ssm_b_re, ssm_b_im, ssm_c_re, ssm_c_im, ssm_d, ssm_w_glu,
           w_mod_kv, b_mod_kv, g_kv_norm, w_dkv, g_ckv, w_kr, w_ukv,
           w_dq, g_cq, w_uq, w_o,
           peer_w_q, peer_keys, peer_u, peer_v,
           w_mod_final, b_mod_final, g_final):
    bsz, seq, d = x_prompt.shape
    bd, dec_seq, _ = x_sample.shape
    assert dec_seq == 1
    depth = w_mod.shape[0]
    n_a = ssm_a_re.shape[0]
    n_groups, n_state = ssm_a_re.shape[1], ssm_a_re.shape[2]
    n_heads = w_ukv.shape[1]
    rope = w_kr.shape[1]
    nope = w_uq.shape[2] // n_heads - rope
    page = cache_kv_latent.shape[1]
    past_len = page_table.shape[1] * page
    sm_scale = 1.0 / math.sqrt(nope + rope)

    c_all = jnp.concatenate([c_prompt, c_sample], axis=0)

    def grouped(m, n):
        parts = jnp.split(m, n, axis=-1)
        return [q[:bsz][:, None, :] for q in parts], [q[bsz:][None] for q in parts]

    mods = [grouped(_mod_linear(c_all, w_mod[l], b_mod[l]), 6) for l in range(depth)]
    mod_kv = grouped(_mod_linear(c_all, w_mod_kv, b_mod_kv), 2)
    mod_f = grouped(_mod_linear(c_all, w_mod_final, b_mod_final), 2)

    s5 = [_s5_prep(ssm_a_re[l], ssm_a_im[l], ssm_log_dt[l], ssm_b_re[l], ssm_b_im[l],
                   ssm_c_re[l], ssm_c_im[l]) for l in range(n_a)]
    glu_w = [ssm_w_glu[l].astype(BF16) for l in range(n_a)]
    peer_w = []
    for l in range(depth):
        wqh, wql = _hi_lo(peer_w_q[l])
        peer_w.append(dict(wqh=wqh, wql=wql, keys=peer_keys[l], u=peer_u[l].astype(BF16),
                           vt=peer_v[l].T.astype(BF16)))
    aw = _attn_weights(w_dkv, g_ckv, w_kr, w_ukv, nope, n_heads)
    qw = [_query_weights(w_dq[j], g_cq[j], w_uq[j], w_o[j], nope, rope, n_heads) for j in range(depth - n_a)]

    pos_p = jnp.arange(seq, dtype=jnp.int32)
    pos_s = past_len + jnp.arange(dec_seq, dtype=jnp.int32)
    tables = []
    for pos, q_scale in ((pos_p, sm_scale * math.log2(math.e)), (pos_s, sm_scale)):
        tables.append(dict(k=_rope_tables(pos, rope), q=_query_tables(pos, nope, rope, q_scale)))

    def gvec(v):
        return v.reshape(1, -1)

    def run(x, grp, h0_re, h0_im):
        is_prompt = grp == 0
        tab = tables[grp]
        ssm_re, ssm_im = [], []
        ckv = kr = kcat = vpad = None
        for layer in range(depth):
            sh1, sc1, ga1, sh2, sc2, ga2 = mods[layer][grp]
            if layer == n_a:
                mk = mod_kv[grp]
                outs = _kv_proj(x, gvec(g_kv_norm), mk[0], mk[1], aw, *tab["k"], heads_out=is_prompt)
                if is_prompt:
                    ckv, kr, kcat, vpad = outs
                else:
                    ckv, kr = outs
            if layer < n_a:
                if is_prompt:
                    z, hr, hi = _s5_prompt(x, gvec(g_norm[layer, 0]), sh1, sc1, s5[layer], gvec(ssm_d[layer]))
                else:
                    z, hr, hi = _s5_step(x, gvec(g_norm[layer, 0]), sh1, sc1, h0_re[layer], h0_im[layer],
                                         s5[layer], gvec(ssm_d[layer]))
                ssm_re.append(hr.reshape(-1, n_groups, n_state))
                ssm_im.append(hi.reshape(-1, n_groups, n_state))
                x = _glu_residual(z, glu_w[layer], x, ga1)
            else:
                lw = qw[layer - n_a]
                if is_prompt:
                    q = _q_proj(x, gvec(g_norm[layer, 0]), sh1, sc1, lw, *tab["q"], n_heads, BF16)
                    o = _flash_attention(q, kcat, vpad, n_heads)
                else:
                    q = _q_proj(x, gvec(g_norm[layer, 0]), sh1, sc1, lw, *tab["q"], n_heads, F32)
                    qlat, qpe = _absorb(q[0], aw["wk_all"], n_heads, nope, rope)
                    rows = qlat.shape[0]
                    o_lat = _decode_attention(qlat.reshape(rows, n_heads, -1), qpe.reshape(rows, n_heads, -1),
                                              ckv.reshape(rows, 1, -1), kr.reshape(rows, 1, -1),
                                              cache_kv_latent, cache_k_rope.transpose(0, 2, 1), page_table)
                    o = _upproj(o_lat.reshape(rows, -1), aw["wv_all"], n_heads)
                x = _out_residual(o, lw["w_o_pad"], x, ga1)
            x = _peer(x, gvec(g_norm[layer, 1]), sh2, sc2, ga2, peer_w[layer])
        shf, scf = mod_f[grp]
        y = _final_norm(x, gvec(g_final), shf, scf)
        return y, jnp.stack(ssm_re), jnp.stack(ssm_im), ckv, kr

    gn = n_groups * n_state
    h0_re = state_ssm_re.reshape(n_a, bd, gn)
    h0_im = state_ssm_im.reshape(n_a, bd, gn)
    y_s, re_s, im_s, ckv_s, kr_s = run(x_sample.reshape(1, bd, d), 1, h0_re, h0_im)
    y_p, re_p, im_p, ckv_p, kr_p = run(x_prompt, 0, None, None)
    return (y_p, y_s.reshape(bd, dec_seq, d), re_p, im_p, ckv_p, kr_p,
            re_s, im_s, ckv_s.reshape(bd, dec_seq, -1), kr_s.reshape(bd, dec_seq, -1))
```
